```python
import math, functools
import jax, jax.numpy as jnp
from jax import lax
import numpy as np

D_MODEL = 4096
BATCH = 4
SEQ = 2048
DEPTH = 2
DEC_BATCH = 8
DEC_SEQ = 4
PAST_LEN = 16384
PAGE_SIZE = 128

HEAD_DIM = 128
D_ATT = D_MODEL // 2
N_HEADS = D_ATT // HEAD_DIM
D_SC = D_MODEL // 4
D_CF = D_MODEL // 4
D_MIX = D_ATT + D_SC + D_CF
SC_WIDTH = 3
CF_WIDTH = 31
MOBA_BLOCK = 256
MOBA_TOPK = 3
MOBA_Q_CHUNK = 8
N_BUCKETS = 32
MAX_DISTANCE = 128
RMS_EPS = 1e-6
LN_EPS = 1e-5
NEG_INF = -1e30
SPLIT_SIZES = (D_ATT, D_ATT, D_ATT, D_ATT, D_SC, D_SC, D_SC, D_SC, D_CF, D_CF, D_CF)
D_IN = sum(SPLIT_SIZES)

kernel_name = "hymba_moba_shortconv_conformer_step"


def rmsnorm(x, g):
    xf = x.astype(jnp.float32)
    y = xf * lax.rsqrt(jnp.mean(xf * xf, axis=-1, keepdims=True) + RMS_EPS)
    return (y * g.astype(jnp.float32)).astype(x.dtype)


def layernorm(x, g, b):
    xf = x.astype(jnp.float32)
    mu = jnp.mean(xf, axis=-1, keepdims=True)
    var = jnp.mean(jnp.square(xf - mu), axis=-1, keepdims=True)
    y = (xf - mu) * lax.rsqrt(var + LN_EPS)
    return (y * g.astype(jnp.float32) + b.astype(jnp.float32)).astype(x.dtype)


def causal_dwconv(u, prev, w):
    width = w.shape[0]
    full = jnp.concatenate([prev.astype(u.dtype), u], axis=1)
    y = lax.conv_general_dilated(full, w[:, None, :].astype(u.dtype), window_strides=(1,),
                                 padding='VALID', dimension_numbers=('NWC', 'WIO', 'NWC'),
                                 feature_group_count=u.shape[-1])
    return y, full[:, full.shape[1] - (width - 1):]


def t5_bucket(dist):
    max_exact = N_BUCKETS // 2
    d = jnp.maximum(dist, 0)
    ratio = jnp.log(jnp.maximum(d, 1).astype(jnp.float32) / max_exact) / math.log(MAX_DISTANCE / max_exact)
    large = max_exact + (ratio * (N_BUCKETS - max_exact)).astype(jnp.int32)
    large = jnp.minimum(large, N_BUCKETS - 1)
    return jnp.where(d < max_exact, d, large)


def to_blocks(*parts):
    b, _, h, d = parts[0].shape
    length = sum(p.shape[1] for p in parts)
    nb = -(-length // MOBA_BLOCK)
    pad = nb * MOBA_BLOCK - length
    rows = list(parts)
    if pad:
        rows.append(jnp.zeros((b, pad, h, d), parts[0].dtype))
    return jnp.concatenate(rows, axis=1).reshape(b, nb, MOBA_BLOCK, h, d)


def moba_attend(q, q_pos, kb, vb, kmean, rel_bias):
    b, h, nq, _ = q.shape
    nb = kb.shape[1]
    own = q_pos // MOBA_BLOCK
    score = jnp.einsum('bhqd,bnhd->bhqn', q.astype(jnp.float32), kmean)
    past = jnp.arange(nb)[None, None, None, :] < own[None, None, :, None]
    score = jnp.where(past, score, NEG_INF)
    _, top = lax.top_k(score, min(MOBA_TOPK, nb))
    valid = top < own[None, None, :, None]
    idx = jnp.concatenate([top, jnp.broadcast_to(own[None, None, :, None], (b, h, nq, 1)).astype(top.dtype)], axis=-1)
    ok = jnp.concatenate([valid, jnp.ones((b, h, nq, 1), bool)], axis=-1)
    n_sel = idx.shape[-1]
    bi = jnp.arange(b)[:, None, None, None]
    hi = jnp.arange(h)[None, :, None, None]
    kg = kb[bi, idx, :, hi].reshape(b, h, nq, n_sel * MOBA_BLOCK, -1)
    vg = vb[bi, idx, :, hi].reshape(b, h, nq, n_sel * MOBA_BLOCK, -1)
    k_pos = (idx[..., None] * MOBA_BLOCK + jnp.arange(MOBA_BLOCK)).reshape(b, h, nq, -1)
    qp = q_pos[None, None, :, None]
    mask = jnp.broadcast_to(ok[..., None], (b, h, nq, n_sel, MOBA_BLOCK)).reshape(b, h, nq, -1) & (k_pos <= qp)
    bias = rel_bias[hi, t5_bucket(qp - k_pos)].astype(jnp.float32)
    logits = jnp.einsum('bhqd,bhqkd->bhqk', q, kg).astype(jnp.float32) + bias
    probs = jax.nn.softmax(jnp.where(mask, logits, NEG_INF), axis=-1)
    return jnp.einsum('bhqk,bhqkd->bhqd', probs.astype(vg.dtype), vg)


def moba_prompt(q, k, v, rel_bias):
    b, t, h, d = q.shape
    kb, vb = to_blocks(k), to_blocks(v)
    kmean = jnp.mean(kb.astype(jnp.float32), axis=2)
    nc = t // MOBA_Q_CHUNK
    qc = (q * (HEAD_DIM ** -0.5)).reshape(b, nc, MOBA_Q_CHUNK, h, d).transpose(1, 0, 3, 2, 4)
    pos = jnp.arange(t, dtype=jnp.int32).reshape(nc, MOBA_Q_CHUNK)
    out = lax.map(lambda a: moba_attend(a[0], a[1], kb, vb, kmean, rel_bias), (qc, pos))
    return out.transpose(1, 0, 3, 2, 4).reshape(b, t, h * d)


def moba_sample(q, k, v, *, past_k, past_v, rel_bias):
    b, t, h, d = q.shape
    kb, vb = to_blocks(past_k, k), to_blocks(past_v, v)
    kmean = jnp.mean(kb.astype(jnp.float32), axis=2)
    pos = past_k.shape[1] + jnp.arange(t, dtype=jnp.int32)
    out = moba_attend((q * (HEAD_DIM ** -0.5)).transpose(0, 2, 1, 3), pos, kb, vb, kmean, rel_bias)
    return out.transpose(0, 2, 1, 3).reshape(b, t, h * d)


def mixer_layer(x, attend, prev_sc, prev_cf, w_in, w_out, g_pre, g_post, sc_w, cf_w, cf_b, cf_ln_g, cf_ln_b):
    b, t, _ = x.shape
    h = rmsnorm(x, g_pre)
    p = h @ w_in
    offsets = [int(o) for o in np.cumsum(SPLIT_SIZES)[:-1]]
    q, k, v, z_att, sc_in, sc_b, sc_c, z_sc, cf_a, cf_g, z_cf = jnp.split(p, offsets, axis=-1)
    q = q.reshape(b, t, N_HEADS, HEAD_DIM)
    k = k.reshape(b, t, N_HEADS, HEAD_DIM)
    v = v.reshape(b, t, N_HEADS, HEAD_DIM)
    att = attend(q, k, v)
    sc_y, sc_state = causal_dwconv(sc_c * sc_in, prev_sc, sc_w)
    sc_out = sc_b * sc_y
    glu = cf_a * jax.nn.sigmoid(cf_g)
    cf_y, cf_state = causal_dwconv(glu, prev_cf, cf_w)
    cf_out = jax.nn.silu(layernorm(cf_y + cf_b, cf_ln_g, cf_ln_b))
    mix = jnp.concatenate([att * jax.nn.silu(z_att), sc_out * jax.nn.silu(z_sc),
                           cf_out * jax.nn.silu(z_cf)], axis=-1)
    y = rmsnorm(mix @ w_out, g_post)
    return x + y, k, v, sc_state, cf_state


def setup_inputs(seed: int = 0) -> dict:
    key = jax.random.key(seed)
    ks = jax.random.split(key, 17)
    f32 = jnp.float32
    n_pages = PAST_LEN // PAGE_SIZE
    n_used = DEC_BATCH * n_pages
    n_pool = n_used + max(1, n_used // 4)
    page_table = jax.random.permutation(ks[0], n_pool)[:n_used].reshape(DEC_BATCH, n_pages).astype(jnp.int32)
    return {
        "x_prompt": jax.random.normal(ks[1], (BATCH, SEQ, D_MODEL), f32),
        "x_sample": jax.random.normal(ks[2], (DEC_BATCH, DEC_SEQ, D_MODEL), f32),
        "cache_k": jax.random.normal(ks[3], (DEPTH, n_pool, PAGE_SIZE, N_HEADS, HEAD_DIM), f32),
        "cache_v": jax.random.normal(ks[4], (DEPTH, n_pool, PAGE_SIZE, N_HEADS, HEAD_DIM), f32),
        "state_sc": jax.random.normal(ks[5], (DEPTH, DEC_BATCH, SC_WIDTH - 1, D_SC), f32),
        "state_cf": 0.5 * jax.random.normal(ks[6], (DEPTH, DEC_BATCH, CF_WIDTH - 1, D_CF), f32),
        "page_table": page_table,
        "w_in": jax.random.normal(ks[7], (DEPTH, D_MODEL, D_IN), f32) * D_MODEL ** -0.5,
        "w_out": jax.random.normal(ks[8], (DEPTH, D_MIX, D_MODEL), f32) * D_MIX ** -0.5,
        "norm_pre": 1.0 + 0.01 * jax.random.normal(ks[9], (DEPTH, D_MODEL), f32),
        "norm_post": 1.0 + 0.01 * jax.random.normal(ks[10], (DEPTH, D_MODEL), f32),
        "sc_w": jax.random.normal(ks[11], (DEPTH, SC_WIDTH, D_SC), f32) * SC_WIDTH ** -0.5,
        "cf_w": jax.random.normal(ks[12], (DEPTH, CF_WIDTH, D_CF), f32) * CF_WIDTH ** -0.5,
        "cf_b": 0.01 * jax.random.normal(ks[13], (DEPTH, D_CF), f32),
        "cf_ln_g": 1.0 + 0.01 * jax.random.normal(ks[14], (DEPTH, D_CF), f32),
        "cf_ln_b": 0.01 * jax.random.normal(ks[15], (DEPTH, D_CF), f32),
        "rel_bias": 0.5 * jax.random.normal(ks[16], (N_HEADS, N_BUCKETS), f32),
    }


def reference(x_prompt, x_sample, cache_k, cache_v, state_sc, state_cf, page_table, w_in, w_out,
              norm_pre, norm_post, sc_w, cf_w, cf_b, cf_ln_g, cf_ln_b, rel_bias):
    b = x_prompt.shape[0]
    db = x_sample.shape[0]
    past_len = page_table.shape[1] * PAGE_SIZE
    zero_sc = jnp.zeros((b, SC_WIDTH - 1, D_SC), x_prompt.dtype)
    zero_cf = jnp.zeros((b, CF_WIDTH - 1, D_CF), x_prompt.dtype)
    attend_prompt = functools.partial(moba_prompt, rel_bias=rel_bias)
    hp, hs = x_prompt, x_sample
    kp_l, vp_l, ks_l, vs_l, scp_l, scs_l, cfp_l, cfs_l = [], [], [], [], [], [], [], []
    for l in range(DEPTH):
        lw = (w_in[l], w_out[l], norm_pre[l], norm_post[l], sc_w[l], cf_w[l], cf_b[l], cf_ln_g[l], cf_ln_b[l])
        hp, kp, vp, scp, cfp = mixer_layer(hp, attend_prompt, zero_sc, zero_cf, *lw)
        past_k = cache_k[l][page_table].reshape(db, past_len, N_HEADS, HEAD_DIM)
        past_v = cache_v[l][page_table].reshape(db, past_len, N_HEADS, HEAD_DIM)
        attend_sample = functools.partial(moba_sample, past_k=past_k, past_v=past_v, rel_bias=rel_bias)
        hs, ks_, vs_, scs, cfs = mixer_layer(hs, attend_sample, state_sc[l], state_cf[l], *lw)
        kp_l.append(kp); vp_l.append(vp); ks_l.append(ks_); vs_l.append(vs_)
        scp_l.append(scp); scs_l.append(scs); cfp_l.append(cfp); cfs_l.append(cfs)
    k_prompt, v_prompt = jnp.stack(kp_l), jnp.stack(vp_l)
    k_sample, v_sample = jnp.stack(ks_l), jnp.stack(vs_l)
    sc_prompt, sc_sample = jnp.stack(scp_l), jnp.stack(scs_l)
    cf_prompt, cf_sample = jnp.stack(cfp_l), jnp.stack(cfs_l)
    return (hp, hs, k_prompt, v_prompt, k_sample, v_sample, sc_prompt, sc_sample, cf_prompt, cf_sample)
```

```python
import functools
import math

import jax
import jax.numpy as jnp
from jax import lax
from jax.experimental import pallas as pl
from jax.experimental.pallas import tpu as pltpu

F32 = jnp.float32
BF16 = jnp.bfloat16
I32 = jnp.int32

HEAD_DIM = 128
PAGE_SIZE = 128
SC_WIDTH = 3
CF_WIDTH = 31
MOBA_BLOCK = 256
MOBA_TOPK = 3
N_BUCKETS = 32
MAX_DISTANCE = 128
RMS_EPS = 1e-6
LN_EPS = 1e-5
NEG_INF = -1e30

PAGES_PER_BLOCK = MOBA_BLOCK // PAGE_SIZE
SUBLANE = 8
SC_HALO = -(-(SC_WIDTH - 1) // SUBLANE) * SUBLANE
CF_HALO = -(-(CF_WIDTH - 1) // SUBLANE) * SUBLANE
LANE = 128

VMEM_LIMIT = 48 * 1024 * 1024


def _params(*sem):
    return pltpu.CompilerParams(dimension_semantics=sem, vmem_limit_bytes=VMEM_LIMIT)


def _silu(x):
    return x * jax.nn.sigmoid(x)


def _rmsnorm_kernel(x_ref, g_ref, o_ref):
    x = x_ref[...]
    ms = jnp.mean(x * x, axis=-1, keepdims=True)
    o_ref[...] = (x * lax.rsqrt(ms + RMS_EPS) * g_ref[...]).astype(o_ref.dtype)


def _rmsnorm(x, g, tm):
    m, d = x.shape
    return pl.pallas_call(
        _rmsnorm_kernel,
        grid=(m // tm,),
        in_specs=[pl.BlockSpec((tm, d), lambda i: (i, 0)),
                  pl.BlockSpec((1, d), lambda i: (0, 0))],
        out_specs=pl.BlockSpec((tm, d), lambda i: (i, 0)),
        out_shape=jax.ShapeDtypeStruct((m, d), BF16),
        compiler_params=_params("parallel"),
        name="rmsnorm_pre",
    )(x, g.reshape(1, d))


def _matmul_kernel(a_ref, w_ref, o_ref):
    o_ref[...] = jnp.dot(a_ref[...], w_ref[...], preferred_element_type=F32).astype(o_ref.dtype)


def _matmul_cols(a, w, col0, ncols, tm, tn):
    m, k = a.shape
    c0 = col0 // tn
    return pl.pallas_call(
        _matmul_kernel,
        grid=(m // tm, ncols // tn),
        in_specs=[pl.BlockSpec((tm, k), lambda i, j: (i, 0)),
                  pl.BlockSpec((k, tn), lambda i, j: (0, c0 + j))],
        out_specs=pl.BlockSpec((tm, tn), lambda i, j: (i, j)),
        out_shape=jax.ShapeDtypeStruct((m, ncols), F32),
        compiler_params=_params("parallel", "parallel"),
        name="in_proj",
    )(a, w)


def _t5_bias(dist, rb_ref, h):
    max_exact = N_BUCKETS // 2
    d = jnp.maximum(dist, 0)
    ratio = jnp.log(jnp.maximum(d, 1).astype(F32) / max_exact) / math.log(MAX_DISTANCE / max_exact)
    large = max_exact + (ratio * (N_BUCKETS - max_exact)).astype(I32)
    large = jnp.minimum(large, N_BUCKETS - 1)
    bucket = jnp.where(d < max_exact, d, large)
    bias = jnp.zeros(dist.shape, F32)
    for b in range(N_BUCKETS):
        bias = jnp.where(bucket == b, rb_ref[h, b], bias)
    return bias


def _split_bf16(x):
    hi = x.astype(BF16)
    lo = (x - hi.astype(F32)).astype(BF16)
    return hi, lo


_NT = (((1,), (1,)), ((), ()))


def _prompt_attn_kernel(rb_ref, q_ref, k_ref, v_ref, z_ref, o_ref,
                        qb_ref, kb_ref, vb_ref, sel_ref, bias_ref):
    h = pl.program_id(1)
    t = q_ref.shape[0]
    nb = t // MOBA_BLOCK
    blk = MOBA_BLOCK

    row = lax.broadcasted_iota(I32, (blk, blk), 0)
    col = lax.broadcasted_iota(I32, (blk, blk), 1)
    bias_ref[0] = _t5_bias(row - col, rb_ref, h)
    bias_ref[1] = _t5_bias(row - col + blk, rb_ref, h)
    far_bias = rb_ref[h, N_BUCKETS - 1]

    kf = k_ref[...]
    qs = q_ref[...] * (HEAD_DIM ** -0.5)
    qh, ql = _split_bf16(qs)
    qb_ref[...] = qh
    kb_ref[...] = kf.astype(BF16)
    vb_ref[...] = v_ref[...].astype(BF16)

    kmean = jnp.mean(kf.reshape(nb, blk, HEAD_DIM), axis=1)
    kh, kl = _split_bf16(kmean)
    score = (lax.dot_general(qh, kh, _NT, preferred_element_type=F32)
             + lax.dot_general(qh, kl, _NT, preferred_element_type=F32)
             + lax.dot_general(ql, kh, _NT, preferred_element_type=F32))
    blk_id = lax.broadcasted_iota(I32, (t, nb), 1)
    own = lax.broadcasted_iota(I32, (t, nb), 0) // blk
    past = blk_id < own
    blk_f = blk_id.astype(F32)
    sc = jnp.where(past, score, NEG_INF)
    picked = jnp.zeros((t, nb), F32)
    for _ in range(min(MOBA_TOPK, nb)):
        best = jnp.max(sc, axis=1, keepdims=True)
        first = jnp.min(jnp.where(sc == best, blk_f, float(nb)), axis=1, keepdims=True)
        hit = blk_f == first
        picked = jnp.where(hit, 1.0, picked)
        sc = jnp.where(hit, -jnp.inf, sc)
    sel_ref[...] = jnp.where(past, picked, 0.0)

    for i in range(nb):
        rows = slice(i * blk, (i + 1) * blk)
        qi = qb_ref[rows, :]
        s = lax.dot_general(qi, kb_ref[rows, :], _NT, preferred_element_type=F32) + bias_ref[0]
        s = jnp.where(col <= row, s, NEG_INF)
        m = jnp.max(s, axis=1, keepdims=True)
        p = jnp.exp(s - m)
        l = jnp.sum(p, axis=1, keepdims=True)
        acc = jnp.dot(p.astype(BF16), vb_ref[rows, :], preferred_element_type=F32)
        for j in range(i):
            cols = slice(j * blk, (j + 1) * blk)
            s = lax.dot_general(qi, kb_ref[cols, :], _NT, preferred_element_type=F32)
            s = s + (bias_ref[1] if j == i - 1 else far_bias)
            s = jnp.where(sel_ref[rows, j:j + 1] > 0.0, s, NEG_INF)
            m_new = jnp.maximum(m, jnp.max(s, axis=1, keepdims=True))
            alpha = jnp.exp(m - m_new)
            p = jnp.exp(s - m_new)
            l = alpha * l + jnp.sum(p, axis=1, keepdims=True)
            acc = alpha * acc + jnp.dot(p.astype(BF16), vb_ref[cols, :], preferred_element_type=F32)
            m = m_new
        o_ref[rows, :] = (acc / l * _silu(z_ref[rows, :])).astype(o_ref.dtype)


def _prompt_attention(q, k, v, pr, rel_bias):
    b, t, d_att = q.shape
    nh = d_att // HEAD_DIM
    nb = t // MOBA_BLOCK
    head = lambda bi, hi: (bi, 0, hi)
    spec = pl.BlockSpec((None, t, HEAD_DIM), head)
    return pl.pallas_call(
        _prompt_attn_kernel,
        grid=(b, nh),
        in_specs=[pl.BlockSpec(memory_space=pltpu.SMEM), spec, spec, spec, spec],
        out_specs=spec,
        out_shape=jax.ShapeDtypeStruct((b, t, d_att), BF16),
        scratch_shapes=[pltpu.VMEM((t, HEAD_DIM), BF16),
                        pltpu.VMEM((t, HEAD_DIM), BF16),
                        pltpu.VMEM((t, HEAD_DIM), BF16),
                        pltpu.VMEM((t, nb), F32),
                        pltpu.VMEM((2, MOBA_BLOCK, MOBA_BLOCK), F32)],
        compiler_params=_params("parallel", "parallel"),
        name="prompt_attn",
    )(rel_bias, q, k, v, pr)


def _conv_kernel(scw_ref, cfw_ref, cfb_ref, lng_ref, lnb_ref, psc_ref, pcf_ref,
                 sc_in_ref, sc_b_ref, sc_c_ref, z_sc_ref, cf_a_ref, cf_g_ref, z_cf_ref,
                 mix_ref, sc_state_ref, cf_state_ref, u_ext, g_ext, y_scr):
    tt, d_sc = sc_in_ref.shape
    d_cf = cf_a_ref.shape[1]

    @pl.when(pl.program_id(1) == 0)
    def _():
        u_ext[0:SC_HALO, :] = psc_ref[...]
        g_ext[0:CF_HALO, :] = pcf_ref[...]

    u_ext[SC_HALO:SC_HALO + tt, :] = sc_c_ref[...] * sc_in_ref[...]
    g_ext[CF_HALO:CF_HALO + tt, :] = cf_a_ref[...] * jax.nn.sigmoid(cf_g_ref[...])

    sc_y = jnp.zeros((tt, d_sc), F32)
    for kk in range(SC_WIDTH):
        off = SC_HALO - (SC_WIDTH - 1) + kk
        sc_y = sc_y + scw_ref[kk:kk + 1, :] * u_ext[off:off + tt, :]
    mix_ref[:, 0:d_sc] = (sc_b_ref[...] * sc_y * _silu(z_sc_ref[...])).astype(mix_ref.dtype)

    for c in range(d_cf // LANE):
        lanes = slice(c * LANE, (c + 1) * LANE)
        acc = jnp.zeros((tt, LANE), F32)
        for kk in range(CF_WIDTH):
            off = CF_HALO - (CF_WIDTH - 1) + kk
            acc = acc + cfw_ref[kk:kk + 1, lanes] * g_ext[off:off + tt, lanes]
        y_scr[:, lanes] = acc + cfb_ref[:, lanes]
    y = y_scr[...]
    mu = jnp.mean(y, axis=-1, keepdims=True)
    var = jnp.mean(jnp.square(y - mu), axis=-1, keepdims=True)
    ln = (y - mu) * lax.rsqrt(var + LN_EPS) * lng_ref[...] + lnb_ref[...]
    mix_ref[:, d_sc:d_sc + d_cf] = (_silu(ln) * _silu(z_cf_ref[...])).astype(mix_ref.dtype)

    sc_tail = u_ext[tt:tt + SC_HALO, :]
    cf_tail = g_ext[tt:tt + CF_HALO, :]
    sc_state_ref[...] = sc_tail
    cf_state_ref[...] = cf_tail
    u_ext[0:SC_HALO, :] = sc_tail
    g_ext[0:CF_HALO, :] = cf_tail


def _conv_mixers(pr, prev_sc, prev_cf, sc_w, cf_w, cf_b, ln_g, ln_b, d_att, d_sc, d_cf, tt):
    b, t, _ = pr.shape
    assert d_sc == d_cf
    c0 = d_att // d_sc
    col = lambda n: pl.BlockSpec((None, tt, d_sc), lambda bi, ti, n=n: (bi, ti, c0 + n))
    full = lambda a: pl.BlockSpec(a.shape, lambda bi, ti: (0,) * a.ndim)
    state = lambda rows, d: pl.BlockSpec((None, rows, d), lambda bi, ti: (bi, 0, 0))
    cf_b, ln_g, ln_b = (a.reshape(1, d_cf) for a in (cf_b, ln_g, ln_b))
    return pl.pallas_call(
        _conv_kernel,
        grid=(b, t // tt),
        in_specs=[full(sc_w), full(cf_w), full(cf_b), full(ln_g), full(ln_b),
                  state(SC_HALO, d_sc), state(CF_HALO, d_cf)] + [col(n) for n in range(7)],
        out_specs=[pl.BlockSpec((None, tt, d_sc + d_cf), lambda bi, ti: (bi, ti, 0)),
                   state(SC_HALO, d_sc), state(CF_HALO, d_cf)],
        out_shape=[jax.ShapeDtypeStruct((b, t, d_sc + d_cf), BF16),
                   jax.ShapeDtypeStruct((b, SC_HALO, d_sc), F32),
                   jax.ShapeDtypeStruct((b, CF_HALO, d_cf), F32)],
        scratch_shapes=[pltpu.VMEM((SC_HALO + tt + SUBLANE, d_sc), F32),
                        pltpu.VMEM((CF_HALO + tt + SUBLANE, d_cf), F32),
                        pltpu.VMEM((tt, d_cf), F32)],
        compiler_params=_params("parallel", "arbitrary"),
        name="conv_mixers",
    )(sc_w, cf_w, cf_b, ln_g, ln_b, prev_sc, prev_cf, *([pr] * 7))


def _pad_history(state, halo):
    return jnp.pad(state, ((0, 0), (halo - state.shape[1], 0), (0, 0)))


def _out_proj_kernel(ma_ref, mc_ref, wa_ref, wc_ref, x_ref, g_ref, o_ref, ss_ref):
    j = pl.program_id(1)
    tn = wa_ref.shape[1]

    @pl.when(j == 0)
    def _():
        ss_ref[...] = jnp.zeros_like(ss_ref)

    y = (jnp.dot(ma_ref[...].astype(BF16), wa_ref[...], preferred_element_type=F32)
         + jnp.dot(mc_ref[...].astype(BF16), wc_ref[...], preferred_element_type=F32))
    ss_ref[...] += jnp.sum(y * y, axis=-1, keepdims=True)
    o_ref[:, pl.ds(pl.multiple_of(j * tn, tn), tn)] = y

    @pl.when(j == pl.num_programs(1) - 1)
    def _():
        inv = lax.rsqrt(ss_ref[...] / o_ref.shape[1] + RMS_EPS)
        o_ref[...] = x_ref[...] + o_ref[...] * inv * g_ref[...]


def _out_proj(mix_att, mix_conv, w_out, x, g, tm, tn):
    m, d = x.shape
    ka, kc = mix_att.shape[1], mix_conv.shape[1]
    assert ka == kc
    return pl.pallas_call(
        _out_proj_kernel,
        grid=(m // tm, d // tn),
        in_specs=[pl.BlockSpec((tm, ka), lambda i, j: (i, 0)),
                  pl.BlockSpec((tm, kc), lambda i, j: (i, 0)),
                  pl.BlockSpec((ka, tn), lambda i, j: (0, j)),
                  pl.BlockSpec((kc, tn), lambda i, j: (1, j)),
                  pl.BlockSpec((tm, d), lambda i, j: (i, 0)),
                  pl.BlockSpec((1, d), lambda i, j: (0, 0))],
        out_specs=pl.BlockSpec((tm, d), lambda i, j: (i, 0)),
        out_shape=jax.ShapeDtypeStruct((m, d), F32),
        scratch_shapes=[pltpu.VMEM((tm, 1), F32)],
        compiler_params=_params("parallel", "arbitrary"),
        name="out_proj",
    )(mix_att, mix_conv, w_out, w_out, x, g.reshape(1, d))


def _block_mean_kernel(pt_ref, *refs):
    page_refs, o_ref = refs[:-1], refs[-1]
    nblk = o_ref.shape[0]
    for n in range(nblk):
        total = jnp.zeros(o_ref.shape[1:], F32)
        for p in range(PAGES_PER_BLOCK):
            total = total + jnp.sum(page_refs[n * PAGES_PER_BLOCK + p][...], axis=0)
        o_ref[n] = total * (1.0 / MOBA_BLOCK)


def _cache_block_means(cache, layer, page_table, group):
    _, _, page, nh, hd = cache.shape
    db, n_pages = page_table.shape
    nb = n_pages // PAGES_PER_BLOCK

    def page_spec(n, p):
        def imap(bi, gi, pt_ref):
            return (layer, pt_ref[bi * n_pages + (gi * group + n) * PAGES_PER_BLOCK + p], 0, 0, 0)
        return pl.BlockSpec((None, None, page, nh, hd), imap)

    specs = [page_spec(n, p) for n in range(group) for p in range(PAGES_PER_BLOCK)]
    return pl.pallas_call(
        _block_mean_kernel,
        grid_spec=pltpu.PrefetchScalarGridSpec(
            num_scalar_prefetch=1,
            grid=(db, nb // group),
            in_specs=specs,
            out_specs=pl.BlockSpec((None, group, nh, hd), lambda bi, gi, pt_ref: (bi, gi, 0, 0)),
        ),
        out_shape=jax.ShapeDtypeStruct((db, nb, nh, hd), F32),
        compiler_params=_params("parallel", "parallel"),
        name="cache_block_means",
    )(page_table.reshape(-1), *([cache] * len(specs)))


def _sample_topk_kernel(q_ref, km_ref, o_ref):
    nq = q_ref.shape[0]
    nh, nb, _ = km_ref.shape
    blk_f = lax.broadcasted_iota(I32, (nb, 1), 0).astype(F32)
    row = lax.broadcasted_iota(I32, (nq, LANE), 0)
    lane = lax.broadcasted_iota(I32, (nq, LANE), 1)
    for h in range(nh):
        cols = slice(h * HEAD_DIM, (h + 1) * HEAD_DIM)
        km = km_ref[h]
        out = jnp.zeros((nq, LANE), F32)
        for qi in range(nq):
            qv = q_ref[qi:qi + 1, cols] * (HEAD_DIM ** -0.5)
            sc = jnp.sum(km * qv, axis=1, keepdims=True)
            for r in range(MOBA_TOPK):
                best = jnp.max(sc, axis=0, keepdims=True)
                first = jnp.min(jnp.where(sc == best, blk_f, float(nb)), axis=0, keepdims=True)
                out = jnp.where((row == qi) & (lane == r), first, out)
                sc = jnp.where(blk_f == first, -jnp.inf, sc)
        o_ref[h] = out.astype(I32)


def _sample_topk(q, kmean):
    db, t, d_att = q.shape
    _, nh, nb, hd = kmean.shape
    out = pl.pallas_call(
        _sample_topk_kernel,
        grid=(db,),
        in_specs=[pl.BlockSpec((None, t, d_att), lambda bi: (bi, 0, 0)),
                  pl.BlockSpec((None, nh, nb, hd), lambda bi: (bi, 0, 0, 0))],
        out_specs=pl.BlockSpec((None, nh, t, LANE), lambda bi: (bi, 0, 0, 0)),
        out_shape=jax.ShapeDtypeStruct((db, nh, t, LANE), I32),
        compiler_params=_params("parallel"),
        name="sample_topk",
    )(q, kmean)
    return out[..., :MOBA_TOPK]


def _sample_attn_kernel(topk_ref, pt_ref, rb_ref, q_ref, kn_ref, vn_ref, z_ref, ck_hbm, cv_hbm,
                        o_ref, kbuf, vbuf, sems, *, layer, n_pages, past_len):
    n_sel = MOBA_TOPK * PAGES_PER_BLOCK
    nq = q_ref.shape[0]
    nh = pl.num_programs(1)
    b, h = pl.program_id(0), pl.program_id(1)

    def chosen_block(qi, s):
        return topk_ref[((b * nh + h) * nq + qi) * MOBA_TOPK + s]

    def page_copies(qi, s, p):
        page = pt_ref[b * n_pages + chosen_block(qi, s) * PAGES_PER_BLOCK + p]
        n = qi * n_sel + s * PAGES_PER_BLOCK + p
        return (pltpu.make_async_copy(ck_hbm.at[layer, page, :, h, :], kbuf.at[n], sems.at[0, n]),
                pltpu.make_async_copy(cv_hbm.at[layer, page, :, h, :], vbuf.at[n], sems.at[1, n]))

    slots = [(qi, s, p) for qi in range(nq) for s in range(MOBA_TOPK) for p in range(PAGES_PER_BLOCK)]
    for slot in slots:
        for cp in page_copies(*slot):
            cp.start()
    for slot in slots:
        for cp in page_copies(*slot):
            cp.wait()

    prow = lax.broadcasted_iota(I32, (PAGE_SIZE, 1), 0)
    nrow = lax.broadcasted_iota(I32, (nq, 1), 0)
    far_bias = rb_ref[h, N_BUCKETS - 1]
    for qi in range(nq):
        q_pos = past_len + qi
        qv = q_ref[qi:qi + 1, :] * (HEAD_DIM ** -0.5)
        logits = []
        for s in range(MOBA_TOPK):
            blk = chosen_block(qi, s)
            for p in range(PAGES_PER_BLOCK):
                kp = kbuf[qi * n_sel + s * PAGES_PER_BLOCK + p]
                lg = jnp.sum(kp * qv, axis=1, keepdims=True)
                first_pos = blk * MOBA_BLOCK + p * PAGE_SIZE
                dist = q_pos - first_pos - prow
                near = q_pos - (first_pos + PAGE_SIZE - 1) < MAX_DISTANCE
                bias = lax.cond(near,
                                lambda dist=dist: _t5_bias(dist, rb_ref, h),
                                lambda: jnp.full((PAGE_SIZE, 1), far_bias, F32))
                logits.append(lg + bias)
        own = jnp.sum(kn_ref[...] * qv, axis=1, keepdims=True) + _t5_bias(qi - nrow, rb_ref, h)
        own = jnp.where(nrow <= qi, own, NEG_INF)
        m = jnp.max(own, axis=0, keepdims=True)
        for lg in logits:
            m = jnp.maximum(m, jnp.max(lg, axis=0, keepdims=True))
        p_own = jnp.exp(own - m)
        l = jnp.sum(p_own, axis=0, keepdims=True)
        acc = jnp.sum(p_own * vn_ref[...], axis=0, keepdims=True)
        for n, lg in enumerate(logits):
            pn = jnp.exp(lg - m)
            l = l + jnp.sum(pn, axis=0, keepdims=True)
            acc = acc + jnp.sum(pn * vbuf[qi * n_sel + n], axis=0, keepdims=True)
        o_ref[qi:qi + 1, :] = acc / l * _silu(z_ref[qi:qi + 1, :])


def _sample_attention(q, k_new, v_new, pr, cache_k, cache_v, layer, page_table, topk, rel_bias):
    db, t, d_att = q.shape
    nh = d_att // HEAD_DIM
    n_pages = page_table.shape[1]
    past_len = n_pages * PAGE_SIZE
    assert past_len % MOBA_BLOCK == 0 and t <= MOBA_BLOCK
    n_buf = t * MOBA_TOPK * PAGES_PER_BLOCK
    head = pl.BlockSpec((None, t, HEAD_DIM), lambda bi, hi, topk_ref, pt_ref: (bi, 0, hi))
    hbm = pl.BlockSpec(memory_space=pl.ANY)
    return pl.pallas_call(
        functools.partial(_sample_attn_kernel, layer=layer, n_pages=n_pages, past_len=past_len),
        grid_spec=pltpu.PrefetchScalarGridSpec(
            num_scalar_prefetch=2,
            grid=(db, nh),
            in_specs=[pl.BlockSpec(memory_space=pltpu.SMEM), head, head, head, head, hbm, hbm],
            out_specs=head,
            scratch_shapes=[pltpu.VMEM((n_buf, PAGE_SIZE, HEAD_DIM), F32),
                            pltpu.VMEM((n_buf, PAGE_SIZE, HEAD_DIM), F32),
                            pltpu.SemaphoreType.DMA((2, n_buf))],
        ),
        out_shape=jax.ShapeDtypeStruct((db, t, d_att), F32),
        compiler_params=_params("arbitrary", "arbitrary"),
        name="sample_attn",
    )(topk.reshape(-1), page_table.reshape(-1), rel_bias, q, k_new, v_new, pr, cache_k, cache_v)


def _tiles(m):
    return (512 if m % 512 == 0 else m), 512


def _layer(x, prev_sc, prev_cf, attend, w_in, w_out, g_pre, g_post, sc_w, cf_w, cf_b, ln_g, ln_b, tt):
    b, t, d = x.shape
    d_sc = sc_w.shape[1]
    d_cf = cf_w.shape[1]
    d_att = (w_in.shape[1] - 4 * d_sc - 3 * d_cf) // 4
    m = b * t
    tm, tn = _tiles(m)
    x2 = x.reshape(m, d)
    h = _rmsnorm(x2, g_pre, min(tm, 256))
    q, k, v = (_matmul_cols(h, w_in, n * d_att, d_att, tm, tn).reshape(b, t, d_att) for n in range(3))
    pr = _matmul_cols(h, w_in, 3 * d_att, w_in.shape[1] - 3 * d_att, tm, tn).reshape(b, t, -1)
    mix_att = attend(q, k, v, pr)
    mix_conv, sc_state, cf_state = _conv_mixers(pr, prev_sc, prev_cf, sc_w, cf_w, cf_b, ln_g, ln_b,
                                                d_att, d_sc, d_cf, tt)
    y = _out_proj(mix_att.reshape(m, d_att), mix_conv.reshape(m, d_sc + d_cf), w_out, x2, g_post,
                  min(tm, 256), 1024)
    return (y.reshape(b, t, d), k, v,
            sc_state[:, SC_HALO - (SC_WIDTH - 1):], cf_state[:, CF_HALO - (CF_WIDTH - 1):])


def kernel(x_prompt, x_sample, cache_k, cache_v, state_sc, state_cf, page_table, w_in, w_out,
           norm_pre, norm_post, sc_w, cf_w, cf_b, cf_ln_g, cf_ln_b, rel_bias):
    depth = w_in.shape[0]
    b, t, _ = x_prompt.shape
    db, dt, _ = x_sample.shape
    d_sc, d_cf = sc_w.shape[2], cf_w.shape[2]
    nh, hd = cache_k.shape[3], cache_k.shape[4]
    w_in_b = w_in.astype(BF16)
    w_out_b = w_out.astype(BF16)
    zero_sc = jnp.zeros((b, SC_HALO, d_sc), F32)
    zero_cf = jnp.zeros((b, CF_HALO, d_cf), F32)

    hp, hs = x_prompt, x_sample
    outs = [[] for _ in range(8)]
    for l in range(depth):
        lw = (w_in_b[l], w_out_b[l], norm_pre[l], norm_post[l], sc_w[l], cf_w[l], cf_b[l],
              cf_ln_g[l], cf_ln_b[l])

        def attend_prompt(q, k, v, pr):
            return _prompt_attention(q, k, v, pr, rel_bias)

        def attend_sample(q, k, v, pr, l=l):
            kmean = _cache_block_means(cache_k, l, page_table, 8)
            topk = _sample_topk(q, kmean.transpose(0, 2, 1, 3))
            return _sample_attention(q, k, v, pr, cache_k, cache_v, l, page_table, topk, rel_bias)

        hp, kp, vp, scp, cfp = _layer(hp, zero_sc, zero_cf, attend_prompt, *lw, tt=256)
        hs, ks, vs, scs, cfs = _layer(hs, _pad_history(state_sc[l], SC_HALO),
                                      _pad_history(state_cf[l], CF_HALO), attend_sample, *lw, tt=dt)
        for lst, val in zip(outs, (kp, vp, ks, vs, scp, scs, cfp, cfs)):
            lst.append(val)

    kp, vp, ks, vs, scp, scs, cfp, cfs = (jnp.stack(o) for o in outs)
    return (hp, hs,
            kp.reshape(depth, b, t, nh, hd), vp.reshape(depth, b, t, nh, hd),
            ks.reshape(depth, db, dt, nh, hd), vs.reshape(depth, db, dt, nh, hd),
            scp, scs, cfp, cfs)
```

```python
import functools
import math

import jax
import jax.numpy as jnp
from jax import lax
from jax.experimental import pallas as pl
from jax.experimental.pallas import tpu as pltpu

F32 = jnp.float32
BF16 = jnp.bfloat16
I32 = jnp.int32

HEAD_DIM = 128
PAGE_SIZE = 128
SC_WIDTH = 3
CF_WIDTH = 31
MOBA_BLOCK = 256
MOBA_TOPK = 3
N_BUCKETS = 32
MAX_DISTANCE = 128
RMS_EPS = 1e-6
LN_EPS = 1e-5
NEG_INF = -1e30

PAGES_PER_BLOCK = MOBA_BLOCK // PAGE_SIZE
SUBLANE = 8
BF16_SUBLANE = 16
LANE = 128
SC_HALO = -(-(SC_WIDTH - 1) // SUBLANE) * SUBLANE
CF_HALO = -(-(CF_WIDTH - 1) // SUBLANE) * SUBLANE
CONV_ROWS = 128

VMEM_LIMIT = 56 * 1024 * 1024


def _params(*sem):
    return pltpu.CompilerParams(dimension_semantics=sem, vmem_limit_bytes=VMEM_LIMIT)


def _silu(x):
    return x * jax.nn.sigmoid(x)


def _rmsnorm_kernel(x_ref, g_ref, o_ref):
    x = x_ref[...]
    ms = jnp.mean(x * x, axis=-1, keepdims=True)
    o_ref[...] = (x * lax.rsqrt(ms + RMS_EPS) * g_ref[...]).astype(o_ref.dtype)


def _rmsnorm(x, g, tm):
    m, d = x.shape
    return pl.pallas_call(
        _rmsnorm_kernel,
        grid=(m // tm,),
        in_specs=[pl.BlockSpec((tm, d), lambda i: (i, 0)),
                  pl.BlockSpec((1, d), lambda i: (0, 0))],
        out_specs=pl.BlockSpec((tm, d), lambda i: (i, 0)),
        out_shape=jax.ShapeDtypeStruct((m, d), BF16),
        compiler_params=_params("parallel"),
        name="rmsnorm_pre",
    )(x, g.reshape(1, d))


def _matmul_kernel(a_ref, w_ref, o_ref):
    o_ref[...] = jnp.dot(a_ref[...], w_ref[...], preferred_element_type=F32).astype(o_ref.dtype)


def _matmul_cols(a, w, layer, col0, ncols, tm, tn):
    m, k = a.shape
    c0 = col0 // tn
    return pl.pallas_call(
        _matmul_kernel,
        grid=(m // tm, ncols // tn),
        in_specs=[pl.BlockSpec((tm, k), lambda i, j: (i, 0)),
                  pl.BlockSpec((None, k, tn), lambda i, j: (layer, 0, c0 + j))],
        out_specs=pl.BlockSpec((tm, tn), lambda i, j: (i, j)),
        out_shape=jax.ShapeDtypeStruct((m, ncols), F32),
        compiler_params=_params("parallel", "parallel"),
        name="in_proj",
    )(a, w)


def _t5_bias(dist, rb_ref, h):
    max_exact = N_BUCKETS // 2
    d = jnp.maximum(dist, 0)
    ratio = jnp.log(jnp.maximum(d, 1).astype(F32) / max_exact) / math.log(MAX_DISTANCE / max_exact)
    large = max_exact + (ratio * (N_BUCKETS - max_exact)).astype(I32)
    large = jnp.minimum(large, N_BUCKETS - 1)
    bucket = jnp.where(d < max_exact, d, large)
    bias = jnp.zeros(dist.shape, F32)
    for b in range(N_BUCKETS):
        bias = jnp.where(bucket == b, rb_ref[h, b], bias)
    return bias


def _split_bf16(x):
    hi = x.astype(BF16)
    lo = (x - hi.astype(F32)).astype(BF16)
    return hi, lo


_NT = (((1,), (1,)), ((), ()))


def _prompt_attn_kernel(rb_ref, q_ref, k_ref, v_ref, z_ref, o_ref,
                        qb_ref, kb_ref, vt_ref, sel_ref, bias_ref, s_ref, p_ref):
    h, b = pl.program_id(0), pl.program_id(1)
    t = q_ref.shape[0]
    nb = t // MOBA_BLOCK
    blk = MOBA_BLOCK
    key = lax.broadcasted_iota(I32, (blk, blk), 0)
    qry = lax.broadcasted_iota(I32, (blk, blk), 1)

    @pl.when(b == 0)
    def _():
        bias_ref[0] = _t5_bias(qry - key, rb_ref, h)
        bias_ref[1] = _t5_bias(qry - key + blk, rb_ref, h)
    far_bias = rb_ref[h, N_BUCKETS - 1]

    kf = k_ref[...]
    qs = q_ref[...] * (HEAD_DIM ** -0.5)
    qh, ql = _split_bf16(qs)
    qb_ref[...] = qh
    kb_ref[...] = kf.astype(BF16)
    for j in range(nb):
        cols = slice(j * blk, (j + 1) * blk)
        vt_ref[:, cols] = v_ref[cols, :].T.astype(BF16)

    kmean = jnp.mean(kf.reshape(nb, blk, HEAD_DIM), axis=1)
    nbp = sel_ref.shape[0]
    kmean = jnp.concatenate([kmean, jnp.zeros((nbp - nb, HEAD_DIM), F32)], axis=0)
    kh, kl = _split_bf16(kmean)
    score = (lax.dot_general(kh, qh, _NT, preferred_element_type=F32)
             + lax.dot_general(kl, qh, _NT, preferred_element_type=F32)
             + lax.dot_general(kh, ql, _NT, preferred_element_type=F32))
    blk_id = lax.broadcasted_iota(I32, (nbp, t), 0)
    own = lax.broadcasted_iota(I32, (nbp, t), 1) // blk
    past = blk_id < own
    blk_f = blk_id.astype(F32)
    sc = jnp.where(past, score, NEG_INF)
    picked = jnp.zeros((nbp, t), F32)
    for _ in range(min(MOBA_TOPK, nb)):
        best = jnp.max(sc, axis=0, keepdims=True)
        first = jnp.min(jnp.where(sc == best, blk_f, float(nbp)), axis=0, keepdims=True)
        hit = blk_f == first
        picked = jnp.where(hit, 1.0, picked)
        sc = jnp.where(hit, -jnp.inf, sc)
    sel_ref[...] = jnp.where(past, picked, 0.0)

    for i in range(nb):
        rows = slice(i * blk, (i + 1) * blk)
        qi = qb_ref[rows, :]
        m = None
        for j in range(i + 1):
            cols = slice(j * blk, (j + 1) * blk)
            s = lax.dot_general(kb_ref[cols, :], qi, _NT, preferred_element_type=F32)
            if j == i:
                s = jnp.where(key <= qry, s + bias_ref[0], NEG_INF)
            else:
                s = s + (bias_ref[1] if j == i - 1 else far_bias)
                s = jnp.where(sel_ref[j:j + 1, rows] > 0.0, s, NEG_INF)
            s_ref[cols, :] = s
            mj = jnp.max(s, axis=0, keepdims=True)
            m = mj if m is None else jnp.maximum(m, mj)
        l = jnp.zeros((1, blk), F32)
        for j in range(i + 1):
            cols = slice(j * blk, (j + 1) * blk)
            p = jnp.exp(s_ref[cols, :] - m)
            l = l + jnp.sum(p, axis=0, keepdims=True)
            p_ref[cols, :] = p.astype(BF16)
        n = (i + 1) * blk
        acc = jnp.dot(vt_ref[:, 0:n], p_ref[0:n, :], preferred_element_type=F32)
        o_ref[rows, :] = ((acc / l).T * _silu(z_ref[rows, :])).astype(o_ref.dtype)


def _prompt_attention(q, k, v, pr, rel_bias):
    b, t, d_att = q.shape
    nh = d_att // HEAD_DIM
    nb = t // MOBA_BLOCK
    nbp = -(-nb // BF16_SUBLANE) * BF16_SUBLANE
    spec = pl.BlockSpec((None, t, HEAD_DIM), lambda hi, bi: (bi, 0, hi))
    return pl.pallas_call(
        _prompt_attn_kernel,
        grid=(nh, b),
        in_specs=[pl.BlockSpec(memory_space=pltpu.SMEM), spec, spec, spec, spec],
        out_specs=spec,
        out_shape=jax.ShapeDtypeStruct((b, t, d_att), BF16),
        scratch_shapes=[pltpu.VMEM((t, HEAD_DIM), BF16),
                        pltpu.VMEM((t, HEAD_DIM), BF16),
                        pltpu.VMEM((HEAD_DIM, t), BF16),
                        pltpu.VMEM((nbp, t), F32),
                        pltpu.VMEM((2, MOBA_BLOCK, MOBA_BLOCK), F32),
                        pltpu.VMEM((t, MOBA_BLOCK), F32),
                        pltpu.VMEM((t, MOBA_BLOCK), BF16)],
        compiler_params=_params("arbitrary", "arbitrary"),
        name="prompt_attn",
    )(rel_bias, q, k, v, pr)


def _conv_kernel(scw_ref, cfw_ref, cfb_ref, lng_ref, lnb_ref, psc_ref, pcf_ref,
                 sc_in_ref, sc_b_ref, sc_c_ref, z_sc_ref, cf_a_ref, cf_g_ref, z_cf_ref,
                 mix_ref, sc_state_ref, cf_state_ref, u_ext, g_ext, y_scr):
    tt, d_sc = sc_in_ref.shape
    d_cf = cf_a_ref.shape[1]

    @pl.when(pl.program_id(1) == 0)
    def _():
        g_ext[...] = jnp.zeros_like(g_ext)
        u_ext[0:SC_HALO, :] = psc_ref[...]
        g_ext[0:CF_HALO, :] = pcf_ref[...]

    u_ext[SC_HALO:SC_HALO + tt, :] = sc_c_ref[...] * sc_in_ref[...]
    g_ext[CF_HALO:CF_HALO + tt, :] = cf_a_ref[...] * jax.nn.sigmoid(cf_g_ref[...])

    sc_y = jnp.zeros((tt, d_sc), F32)
    for kk in range(SC_WIDTH):
        off = SC_HALO - (SC_WIDTH - 1) + kk
        sc_y = sc_y + scw_ref[kk:kk + 1, :] * u_ext[off:off + tt, :]
    mix_ref[:, 0:d_sc] = (sc_b_ref[...] * sc_y * _silu(z_sc_ref[...])).astype(mix_ref.dtype)

    first = CF_HALO - (CF_WIDTH - 1)
    sub = min(tt, CONV_ROWS)
    for c in range(d_cf // LANE):
        lanes = slice(c * LANE, (c + 1) * LANE)
        for r0 in range(0, tt, sub):
            acc = jnp.zeros((sub, LANE), F32)
            for rem in range(SUBLANE):
                part = None
                for base in range(0, CF_HALO + SUBLANE, SUBLANE):
                    kk = base + rem - first
                    if 0 <= kk < CF_WIDTH:
                        term = (cfw_ref[kk:kk + 1, lanes]
                                * g_ext[r0 + base:r0 + base + sub + SUBLANE, lanes])
                        part = term if part is None else part + term
                if part is not None:
                    acc = acc + part[rem:rem + sub, :]
            y_scr[r0:r0 + sub, lanes] = acc + cfb_ref[:, lanes]
    y = y_scr[...]
    mu = jnp.mean(y, axis=-1, keepdims=True)
    var = jnp.mean(jnp.square(y - mu), axis=-1, keepdims=True)
    ln = (y - mu) * lax.rsqrt(var + LN_EPS) * lng_ref[...] + lnb_ref[...]
    mix_ref[:, d_sc:d_sc + d_cf] = (_silu(ln) * _silu(z_cf_ref[...])).astype(mix_ref.dtype)

    sc_tail = u_ext[tt:tt + SC_HALO, :]
    cf_tail = g_ext[tt:tt + CF_HALO, :]
    sc_state_ref[...] = sc_tail
    cf_state_ref[...] = cf_tail
    u_ext[0:SC_HALO, :] = sc_tail
    g_ext[0:CF_HALO, :] = cf_tail


def _conv_mixers(pr, prev_sc, prev_cf, sc_w, cf_w, cf_b, ln_g, ln_b, d_att, d_sc, d_cf, tt):
    b, t, _ = pr.shape
    assert d_sc == d_cf and tt % min(tt, CONV_ROWS) == 0
    c0 = d_att // d_sc
    col = lambda n: pl.BlockSpec((None, tt, d_sc), lambda bi, ti, n=n: (bi, ti, c0 + n))
    full = lambda a: pl.BlockSpec(a.shape, lambda bi, ti: (0,) * a.ndim)
    state = lambda rows, d: pl.BlockSpec((None, rows, d), lambda bi, ti: (bi, 0, 0))
    cf_b, ln_g, ln_b = (a.reshape(1, d_cf) for a in (cf_b, ln_g, ln_b))
    ext_rows = lambda halo: -(-(halo + tt) // SUBLANE) * SUBLANE + SUBLANE
    return pl.pallas_call(
        _conv_kernel,
        grid=(b, t // tt),
        in_specs=[full(sc_w), full(cf_w), full(cf_b), full(ln_g), full(ln_b),
                  state(SC_HALO, d_sc), state(CF_HALO, d_cf)] + [col(n) for n in range(7)],
        out_specs=[pl.BlockSpec((None, tt, d_sc + d_cf), lambda bi, ti: (bi, ti, 0)),
                   state(SC_HALO, d_sc), state(CF_HALO, d_cf)],
        out_shape=[jax.ShapeDtypeStruct((b, t, d_sc + d_cf), BF16),
                   jax.ShapeDtypeStruct((b, SC_HALO, d_sc), F32),
                   jax.ShapeDtypeStruct((b, CF_HALO, d_cf), F32)],
        scratch_shapes=[pltpu.VMEM((ext_rows(SC_HALO), d_sc), F32),
                        pltpu.VMEM((ext_rows(CF_HALO), d_cf), F32),
                        pltpu.VMEM((tt, d_cf), F32)],
        compiler_params=_params("parallel", "arbitrary"),
        name="conv_mixers",
    )(sc_w, cf_w, cf_b, ln_g, ln_b, prev_sc, prev_cf, *([pr] * 7))


def _pad_history(state, halo):
    return jnp.pad(state, ((0, 0), (halo - state.shape[1], 0), (0, 0)))


def _out_proj_kernel(ma_ref, mc_ref, wa_ref, wc_ref, g_ref, x_hbm, o_ref, ss_ref, x_buf, x_sem):
    i, j = pl.program_id(0), pl.program_id(1)
    tm = o_ref.shape[0]
    tn = wa_ref.shape[1]

    def x_copy():
        return pltpu.make_async_copy(x_hbm.at[pl.ds(pl.multiple_of(i * tm, tm), tm), :], x_buf, x_sem)

    @pl.when(j == 0)
    def _():
        ss_ref[...] = jnp.zeros_like(ss_ref)
        x_copy().start()

    y = (jnp.dot(ma_ref[...].astype(BF16), wa_ref[...], preferred_element_type=F32)
         + jnp.dot(mc_ref[...].astype(BF16), wc_ref[...], preferred_element_type=F32))
    ss_ref[...] += jnp.sum(y * y, axis=-1, keepdims=True)
    o_ref[:, pl.ds(pl.multiple_of(j * tn, tn), tn)] = y

    @pl.when(j == pl.num_programs(1) - 1)
    def _():
        x_copy().wait()
        inv = lax.rsqrt(ss_ref[...] / o_ref.shape[1] + RMS_EPS)
        o_ref[...] = x_buf[...] + o_ref[...] * inv * g_ref[...]


def _out_proj(mix_att, mix_conv, w_out, layer, x, g, tm, tn):
    m, d = x.shape
    ka, kc = mix_att.shape[1], mix_conv.shape[1]
    assert ka == kc
    return pl.pallas_call(
        _out_proj_kernel,
        grid=(m // tm, d // tn),
        in_specs=[pl.BlockSpec((tm, ka), lambda i, j: (i, 0)),
                  pl.BlockSpec((tm, kc), lambda i, j: (i, 0)),
                  pl.BlockSpec((None, ka, tn), lambda i, j: (layer, 0, j)),
                  pl.BlockSpec((None, kc, tn), lambda i, j: (layer, 1, j)),
                  pl.BlockSpec((1, d), lambda i, j: (0, 0)),
                  pl.BlockSpec(memory_space=pl.ANY)],
        out_specs=pl.BlockSpec((tm, d), lambda i, j: (i, 0)),
        out_shape=jax.ShapeDtypeStruct((m, d), F32),
        scratch_shapes=[pltpu.VMEM((tm, 1), F32),
                        pltpu.VMEM((tm, d), F32),
                        pltpu.SemaphoreType.DMA(())],
        compiler_params=_params("arbitrary", "arbitrary"),
        name="out_proj",
    )(mix_att, mix_conv, w_out, w_out, g.reshape(1, d), x)


def _block_mean_kernel(pt_ref, *refs):
    page_refs, o_ref = refs[:-1], refs[-1]
    nblk = o_ref.shape[0]
    for n in range(nblk):
        total = jnp.zeros(o_ref.shape[1:], F32)
        for p in range(PAGES_PER_BLOCK):
            total = total + jnp.sum(page_refs[n * PAGES_PER_BLOCK + p][...], axis=0)
        o_ref[n] = total * (1.0 / MOBA_BLOCK)


def _cache_block_means(cache, layer, page_table, group):
    _, _, page, nh, hd = cache.shape
    db, n_pages = page_table.shape
    nb = n_pages // PAGES_PER_BLOCK

    def page_spec(n, p):
        def imap(bi, gi, pt_ref):
            return (layer, pt_ref[bi * n_pages + (gi * group + n) * PAGES_PER_BLOCK + p], 0, 0, 0)
        return pl.BlockSpec((None, None, page, nh, hd), imap)

    specs = [page_spec(n, p) for n in range(group) for p in range(PAGES_PER_BLOCK)]
    return pl.pallas_call(
        _block_mean_kernel,
        grid_spec=pltpu.PrefetchScalarGridSpec(
            num_scalar_prefetch=1,
            grid=(db, nb // group),
            in_specs=specs,
            out_specs=pl.BlockSpec((None, group, nh, hd), lambda bi, gi, pt_ref: (bi, gi, 0, 0)),
        ),
        out_shape=jax.ShapeDtypeStruct((db, nb, nh, hd), F32),
        compiler_params=_params("parallel", "parallel"),
        name="cache_block_means",
    )(page_table.reshape(-1), *([cache] * len(specs)))


def _sample_topk_kernel(q_ref, km_ref, o_ref):
    nq = q_ref.shape[0]
    nh, nb, _ = km_ref.shape
    blk_f = lax.broadcasted_iota(I32, (nb, 1), 0).astype(F32)
    row = lax.broadcasted_iota(I32, (nq, LANE), 0)
    lane = lax.broadcasted_iota(I32, (nq, LANE), 1)
    for h in range(nh):
        cols = slice(h * HEAD_DIM, (h + 1) * HEAD_DIM)
        km = km_ref[h]
        out = jnp.zeros((nq, LANE), F32)
        for qi in range(nq):
            qv = q_ref[qi:qi + 1, cols] * (HEAD_DIM ** -0.5)
            sc = jnp.sum(km * qv, axis=1, keepdims=True)
            for r in range(MOBA_TOPK):
                best = jnp.max(sc, axis=0, keepdims=True)
                first = jnp.min(jnp.where(sc == best, blk_f, float(nb)), axis=0, keepdims=True)
                out = jnp.where((row == qi) & (lane == r), first, out)
                sc = jnp.where(blk_f == first, -jnp.inf, sc)
        o_ref[h] = out.astype(I32)


def _sample_topk(q, kmean):
    db, t, d_att = q.shape
    _, nh, nb, hd = kmean.shape
    out = pl.pallas_call(
        _sample_topk_kernel,
        grid=(db,),
        in_specs=[pl.BlockSpec((None, t, d_att), lambda bi: (bi, 0, 0)),
                  pl.BlockSpec((None, nh, nb, hd), lambda bi: (bi, 0, 0, 0))],
        out_specs=pl.BlockSpec((None, nh, t, LANE), lambda bi: (bi, 0, 0, 0)),
        out_shape=jax.ShapeDtypeStruct((db, nh, t, LANE), I32),
        compiler_params=_params("parallel"),
        name="sample_topk",
    )(q, kmean)
    return out[..., :MOBA_TOPK]


def _sample_attn_kernel(topk_ref, pt_ref, rb_ref, q_ref, kn_ref, vn_ref, z_ref, ck_hbm, cv_hbm,
                        o_ref, kbuf, vbuf, sems, *, layer, n_pages, past_len):
    n_sel = MOBA_TOPK * PAGES_PER_BLOCK
    nq = q_ref.shape[0]
    nh = pl.num_programs(1)
    h = pl.program_id(1)
    step = pl.program_id(0) * nh + h
    n_steps = pl.num_programs(0) * nh
    half = step % 2

    def chosen_block(g, qi, s):
        return topk_ref[(g * nq + qi) * MOBA_TOPK + s]

    slots = [(qi, s, p) for qi in range(nq) for s in range(MOBA_TOPK) for p in range(PAGES_PER_BLOCK)]

    def page_copies(g, buf_half, qi, s, p):
        page = pt_ref[(g // nh) * n_pages + chosen_block(g, qi, s) * PAGES_PER_BLOCK + p]
        n = qi * n_sel + s * PAGES_PER_BLOCK + p
        head = g % nh
        return (pltpu.make_async_copy(ck_hbm.at[layer, page, :, head, :], kbuf.at[buf_half, n],
                                      sems.at[0, buf_half, n]),
                pltpu.make_async_copy(cv_hbm.at[layer, page, :, head, :], vbuf.at[buf_half, n],
                                      sems.at[1, buf_half, n]))

    def start_all(g, buf_half):
        for slot in slots:
            for cp in page_copies(g, buf_half, *slot):
                cp.start()

    @pl.when(step == 0)
    def _():
        start_all(step, half)

    @pl.when(step + 1 < n_steps)
    def _():
        start_all(step + 1, 1 - half)

    for slot in slots:
        for cp in page_copies(step, half, *slot):
            cp.wait()

    prow = lax.broadcasted_iota(I32, (PAGE_SIZE, 1), 0)
    nrow = lax.broadcasted_iota(I32, (nq, 1), 0)
    far_bias = rb_ref[h, N_BUCKETS - 1]
    for qi in range(nq):
        q_pos = past_len + qi
        qv = q_ref[qi:qi + 1, :] * (HEAD_DIM ** -0.5)
        logits = []
        for s in range(MOBA_TOPK):
            blk = chosen_block(step, qi, s)
            for p in range(PAGES_PER_BLOCK):
                kp = kbuf[half, qi * n_sel + s * PAGES_PER_BLOCK + p]
                lg = jnp.sum(kp * qv, axis=1, keepdims=True)
                first_pos = blk * MOBA_BLOCK + p * PAGE_SIZE
                dist = q_pos - first_pos - prow
                near = q_pos - (first_pos + PAGE_SIZE - 1) < MAX_DISTANCE
                bias = lax.cond(near,
                                lambda dist=dist: _t5_bias(dist, rb_ref, h),
                                lambda: jnp.full((PAGE_SIZE, 1), far_bias, F32))
                logits.append(lg + bias)
        own = jnp.sum(kn_ref[...] * qv, axis=1, keepdims=True) + _t5_bias(qi - nrow, rb_ref, h)
        own = jnp.where(nrow <= qi, own, NEG_INF)
        m = jnp.max(own, axis=0, keepdims=True)
        for lg in logits:
            m = jnp.maximum(m, jnp.max(lg, axis=0, keepdims=True))
        p_own = jnp.exp(own - m)
        l = jnp.sum(p_own, axis=0, keepdims=True)
        acc = jnp.sum(p_own * vn_ref[...], axis=0, keepdims=True)
        for n, lg in enumerate(logits):
            pn = jnp.exp(lg - m)
            l = l + jnp.sum(pn, axis=0, keepdims=True)
            acc = acc + jnp.sum(pn * vbuf[half, qi * n_sel + n], axis=0, keepdims=True)
        o_ref[qi:qi + 1, :] = acc / l * _silu(z_ref[qi:qi + 1, :])


def _sample_attention(q, k_new, v_new, pr, cache_k, cache_v, layer, page_table, topk, rel_bias):
    db, t, d_att = q.shape
    nh = d_att // HEAD_DIM
    n_pages = page_table.shape[1]
    past_len = n_pages * PAGE_SIZE
    assert past_len % MOBA_BLOCK == 0 and t <= MOBA_BLOCK
    n_buf = t * MOBA_TOPK * PAGES_PER_BLOCK
    head = pl.BlockSpec((None, t, HEAD_DIM), lambda bi, hi, topk_ref, pt_ref: (bi, 0, hi))
    hbm = pl.BlockSpec(memory_space=pl.ANY)
    return pl.pallas_call(
        functools.partial(_sample_attn_kernel, layer=layer, n_pages=n_pages, past_len=past_len),
        grid_spec=pltpu.PrefetchScalarGridSpec(
            num_scalar_prefetch=2,
            grid=(db, nh),
            in_specs=[pl.BlockSpec(memory_space=pltpu.SMEM), head, head, head, head, hbm, hbm],
            out_specs=head,
            scratch_shapes=[pltpu.VMEM((2, n_buf, PAGE_SIZE, HEAD_DIM), F32),
                            pltpu.VMEM((2, n_buf, PAGE_SIZE, HEAD_DIM), F32),
                            pltpu.SemaphoreType.DMA((2, 2, n_buf))],
        ),
        out_shape=jax.ShapeDtypeStruct((db, t, d_att), F32),
        compiler_params=_params("arbitrary", "arbitrary"),
        name="sample_attn",
    )(topk.reshape(-1), page_table.reshape(-1), rel_bias, q, k_new, v_new, pr, cache_k, cache_v)


def _row_tiles(m):
    return (1024 if m % 1024 == 0 else m), (512 if m % 512 == 0 else m)


def _layer(x, prev_sc, prev_cf, attend, layer, w_in, w_out, g_pre, g_post, sc_w, cf_w, cf_b,
           ln_g, ln_b, tt):
    b, t, d = x.shape
    d_sc = sc_w.shape[1]
    d_cf = cf_w.shape[1]
    d_in = w_in.shape[2]
    d_att = (d_in - 4 * d_sc - 3 * d_cf) // 4
    m = b * t
    tm_in, tm_out = _row_tiles(m)
    tn = 512
    x2 = x.reshape(m, d)
    h = _rmsnorm(x2, g_pre, min(tm_out, 256))
    q, k, v = (_matmul_cols(h, w_in, layer, n * d_att, d_att, tm_in, tn).reshape(b, t, d_att)
               for n in range(3))
    pr = _matmul_cols(h, w_in, layer, 3 * d_att, d_in - 3 * d_att, tm_in, tn).reshape(b, t, -1)
    mix_att = attend(q, k, v, pr)
    mix_conv, sc_state, cf_state = _conv_mixers(pr, prev_sc, prev_cf, sc_w, cf_w, cf_b, ln_g, ln_b,
                                                d_att, d_sc, d_cf, tt)
    y = _out_proj(mix_att.reshape(m, d_att), mix_conv.reshape(m, d_sc + d_cf), w_out, layer, x2,
                  g_post, tm_out, tn)
    return (y.reshape(b, t, d), k, v,
            sc_state[:, SC_HALO - (SC_WIDTH - 1):], cf_state[:, CF_HALO - (CF_WIDTH - 1):])


def kernel(x_prompt, x_sample, cache_k, cache_v, state_sc, state_cf, page_table, w_in, w_out,
           norm_pre, norm_post, sc_w, cf_w, cf_b, cf_ln_g, cf_ln_b, rel_bias):
    depth = w_in.shape[0]
    b, t, _ = x_prompt.shape
    db, dt, _ = x_sample.shape
    d_sc, d_cf = sc_w.shape[2], cf_w.shape[2]
    nh, hd = cache_k.shape[3], cache_k.shape[4]
    w_in_b = w_in.astype(BF16)
    w_out_b = w_out.astype(BF16)
    zero_sc = jnp.zeros((b, SC_HALO, d_sc), F32)
    zero_cf = jnp.zeros((b, CF_HALO, d_cf), F32)

    hp, hs = x_prompt, x_sample
    outs = [[] for _ in range(8)]
    for l in range(depth):
        lw = (l, w_in_b, w_out_b, norm_pre[l], norm_post[l], sc_w[l], cf_w[l], cf_b[l],
              cf_ln_g[l], cf_ln_b[l])

        def attend_prompt(q, k, v, pr):
            return _prompt_attention(q, k, v, pr, rel_bias)

        def attend_sample(q, k, v, pr, l=l):
            kmean = _cache_block_means(cache_k, l, page_table, 8)
            topk = _sample_topk(q, kmean.transpose(0, 2, 1, 3))
            return _sample_attention(q, k, v, pr, cache_k, cache_v, l, page_table, topk, rel_bias)

        hp, kp, vp, scp, cfp = _layer(hp, zero_sc, zero_cf, attend_prompt, *lw, tt=256)
        hs, ks, vs, scs, cfs = _layer(hs, _pad_history(state_sc[l], SC_HALO),
                                      _pad_history(state_cf[l], CF_HALO), attend_sample, *lw, tt=dt)
        for lst, val in zip(outs, (kp, vp, ks, vs, scp, scs, cfp, cfs)):
            lst.append(val)

    kp, vp, ks, vs, scp, scs, cfp, cfs = (jnp.stack(o) for o in outs)
    return (hp, hs,
            kp.reshape(depth, b, t, nh, hd), vp.reshape(depth, b, t, nh, hd),
            ks.reshape(depth, db, dt, nh, hd), vs.reshape(depth, db, dt, nh, hd),
            scp, scs, cfp, cfs)
```

```python
import functools
import math

import jax
import jax.numpy as jnp
from jax import lax
from jax.experimental import pallas as pl
from jax.experimental.pallas import tpu as pltpu

F32 = jnp.float32
BF16 = jnp.bfloat16
I32 = jnp.int32

HEAD_DIM = 128
PAGE_SIZE = 128
SC_WIDTH = 3
CF_WIDTH = 31
MOBA_BLOCK = 256
MOBA_TOPK = 3
N_BUCKETS = 32
MAX_DISTANCE = 128
RMS_EPS = 1e-6
LN_EPS = 1e-5
NEG_INF = -1e30

PAGES_PER_BLOCK = MOBA_BLOCK // PAGE_SIZE
SUBLANE = 8
BF16_SUBLANE = 16
LANE = 128
SC_HALO = -(-(SC_WIDTH - 1) // SUBLANE) * SUBLANE
CF_HALO = -(-(CF_WIDTH - 1) // SUBLANE) * SUBLANE
CONV_ROWS = 128

VMEM_LIMIT = 56 * 1024 * 1024


def _params(*sem):
    return pltpu.CompilerParams(dimension_semantics=sem, vmem_limit_bytes=VMEM_LIMIT)


def _silu(x):
    return x * jax.nn.sigmoid(x)


def _rmsnorm_kernel(x_ref, g_ref, o_ref):
    x = x_ref[...]
    ms = jnp.mean(x * x, axis=-1, keepdims=True)
    o_ref[...] = (x * lax.rsqrt(ms + RMS_EPS) * g_ref[...]).astype(o_ref.dtype)


def _rmsnorm(x, g, tm):
    m, d = x.shape
    return pl.pallas_call(
        _rmsnorm_kernel,
        grid=(m // tm,),
        in_specs=[pl.BlockSpec((tm, d), lambda i: (i, 0)),
                  pl.BlockSpec((1, d), lambda i: (0, 0))],
        out_specs=pl.BlockSpec((tm, d), lambda i: (i, 0)),
        out_shape=jax.ShapeDtypeStruct((m, d), BF16),
        compiler_params=_params("parallel"),
        name="rmsnorm_pre",
    )(x, g.reshape(1, d))


def _matmul_kernel(a_ref, w_ref, *rest, layer=None):
    o_ref = rest[-1]

    def product():
        o_ref[...] = jnp.dot(a_ref[...], w_ref[...].astype(BF16), preferred_element_type=F32)

    if layer is None:
        product()
    else:
        slab = pl.program_id(2)
        pl.when(slab == layer)(product)

        @pl.when(slab != layer)
        def _():
            o_ref[...] = jnp.zeros_like(o_ref)


def _matmul_cols(a, w, layer, col0, ncols, tm, tn, stack=None):
    m, k = a.shape
    depth = w.shape[0]
    c0 = col0 // tn
    grid = (m // tm, ncols // tn)
    in_specs = [pl.BlockSpec((tm, k), lambda i, j, *_: (i, 0)),
                pl.BlockSpec((None, k, tn), lambda i, j, *_: (layer, 0, c0 + j))]
    args = [a, w]
    aliases = {}
    body = _matmul_kernel
    if stack is None:
        out_spec = pl.BlockSpec((tm, tn), lambda i, j: (i, j))
        out_shape = jax.ShapeDtypeStruct((m, ncols), F32)
    else:
        out_shape = jax.ShapeDtypeStruct((depth, m, ncols), F32)
        if isinstance(stack, tuple):
            grid = grid + (depth,)
            out_spec = pl.BlockSpec((None, tm, tn), lambda i, j, s: (s, i, j))
            body = functools.partial(_matmul_kernel, layer=layer)
        else:
            out_spec = pl.BlockSpec((None, tm, tn), lambda i, j: (layer, i, j))
            in_specs.append(pl.BlockSpec(memory_space=pl.ANY))
            args.append(stack)
            aliases = {2: 0}
    return pl.pallas_call(
        body,
        grid=grid,
        in_specs=in_specs,
        out_specs=out_spec,
        out_shape=out_shape,
        input_output_aliases=aliases,
        compiler_params=_params(*(["arbitrary"] * len(grid))),
        name="in_proj",
    )(*args)


def _t5_bias(dist, rb_ref, h):
    max_exact = N_BUCKETS // 2
    d = jnp.maximum(dist, 0)
    ratio = jnp.log(jnp.maximum(d, 1).astype(F32) / max_exact) / math.log(MAX_DISTANCE / max_exact)
    large = max_exact + (ratio * (N_BUCKETS - max_exact)).astype(I32)
    large = jnp.minimum(large, N_BUCKETS - 1)
    bucket = jnp.where(d < max_exact, d, large)
    bias = jnp.zeros(dist.shape, F32)
    for b in range(N_BUCKETS):
        bias = jnp.where(bucket == b, rb_ref[h, b], bias)
    return bias


def _split_bf16(x):
    hi = x.astype(BF16)
    lo = (x - hi.astype(F32)).astype(BF16)
    return hi, lo


_NT = (((1,), (1,)), ((), ()))


def _block_means(page_refs, o_ref):
    for n in range(o_ref.shape[0]):
        total = jnp.zeros(o_ref.shape[1:], F32)
        for p in range(PAGES_PER_BLOCK):
            total = total + jnp.sum(page_refs[n * PAGES_PER_BLOCK + p][...], axis=0)
        o_ref[n] = total * (1.0 / MOBA_BLOCK)


def _prompt_attn_kernel(pt_ref, rb_ref, q_ref, k_ref, v_ref, z_ref, *refs):
    qb_ref, kb_ref, vt_ref, sel_ref, bias_ref, s_ref, p_ref = refs[-7:]
    o_ref, km_ref = refs[-9:-7]
    _block_means(refs[:-9], km_ref)

    h, b = pl.program_id(0), pl.program_id(1)
    t = q_ref.shape[0]
    nb = t // MOBA_BLOCK
    blk = MOBA_BLOCK
    key = lax.broadcasted_iota(I32, (blk, blk), 0)
    qry = lax.broadcasted_iota(I32, (blk, blk), 1)

    @pl.when(b == 0)
    def _():
        bias_ref[0] = _t5_bias(qry - key, rb_ref, h)
        bias_ref[1] = _t5_bias(qry - key + blk, rb_ref, h)
    far_bias = rb_ref[h, N_BUCKETS - 1]

    kf = k_ref[...]
    qs = q_ref[...] * (HEAD_DIM ** -0.5)
    qh, ql = _split_bf16(qs)
    qb_ref[...] = qh
    kb_ref[...] = kf.astype(BF16)
    for j in range(nb):
        cols = slice(j * blk, (j + 1) * blk)
        vt_ref[:, cols] = v_ref[cols, :].T.astype(BF16)

    kmean = jnp.mean(kf.reshape(nb, blk, HEAD_DIM), axis=1)
    nbp = sel_ref.shape[0]
    kmean = jnp.concatenate([kmean, jnp.zeros((nbp - nb, HEAD_DIM), F32)], axis=0)
    kh, kl = _split_bf16(kmean)
    score = (lax.dot_general(kh, qh, _NT, preferred_element_type=F32)
             + lax.dot_general(kl, qh, _NT, preferred_element_type=F32)
             + lax.dot_general(kh, ql, _NT, preferred_element_type=F32))
    blk_id = lax.broadcasted_iota(I32, (nbp, t), 0)
    own = lax.broadcasted_iota(I32, (nbp, t), 1) // blk
    past = blk_id < own
    blk_f = blk_id.astype(F32)
    sc = jnp.where(past, score, NEG_INF)
    picked = jnp.zeros((nbp, t), F32)
    for _ in range(min(MOBA_TOPK, nb)):
        best = jnp.max(sc, axis=0, keepdims=True)
        first = jnp.min(jnp.where(sc == best, blk_f, float(nbp)), axis=0, keepdims=True)
        hit = blk_f == first
        picked = jnp.where(hit, 1.0, picked)
        sc = jnp.where(hit, -jnp.inf, sc)
    sel_ref[...] = jnp.where(past, picked, 0.0)

    for i in range(nb):
        rows = slice(i * blk, (i + 1) * blk)
        qi = qb_ref[rows, :]
        m = None
        for j in range(i + 1):
            cols = slice(j * blk, (j + 1) * blk)
            s = lax.dot_general(kb_ref[cols, :], qi, _NT, preferred_element_type=F32)
            if j == i:
                s = jnp.where(key <= qry, s + bias_ref[0], NEG_INF)
            else:
                s = s + (bias_ref[1] if j == i - 1 else far_bias)
                s = jnp.where(sel_ref[j:j + 1, rows] > 0.0, s, NEG_INF)
            s_ref[cols, :] = s
            mj = jnp.max(s, axis=0, keepdims=True)
            m = mj if m is None else jnp.maximum(m, mj)
        l = jnp.zeros((1, blk), F32)
        for j in range(i + 1):
            cols = slice(j * blk, (j + 1) * blk)
            p = jnp.exp(s_ref[cols, :] - m)
            l = l + jnp.sum(p, axis=0, keepdims=True)
            p_ref[cols, :] = p.astype(BF16)
        n = (i + 1) * blk
        acc = jnp.dot(vt_ref[:, 0:n], p_ref[0:n, :], preferred_element_type=F32)
        o_ref[rows, :] = ((acc / l).T * _silu(z_ref[rows, :])).astype(o_ref.dtype)


def _prompt_attention(q, k_stack, v_stack, pr, rel_bias, layer, cache, page_table):
    b, t, d_att = q.shape
    nh = d_att // HEAD_DIM
    nb = t // MOBA_BLOCK
    nbp = -(-nb // BF16_SUBLANE) * BF16_SUBLANE
    _, _, page, ch, cd = cache.shape
    db, n_pages = page_table.shape
    pages_per_step = db * n_pages // (nh * b)
    assert pages_per_step * nh * b == db * n_pages and n_pages % pages_per_step == 0
    assert pages_per_step % PAGES_PER_BLOCK == 0
    steps_per_seq = n_pages // pages_per_step
    blocks_per_step = pages_per_step // PAGES_PER_BLOCK

    def page_spec(n):
        def imap(hi, bi, pt_ref):
            return (layer, pt_ref[(hi * b + bi) * pages_per_step + n], 0, 0, 0)
        return pl.BlockSpec((None, None, page, ch, cd), imap)

    def km_map(hi, bi, pt_ref):
        step = hi * b + bi
        return (step // steps_per_seq, step % steps_per_seq, 0, 0)

    spec = pl.BlockSpec((None, t, HEAD_DIM), lambda hi, bi, pt_ref: (bi, 0, hi))
    kv_spec = pl.BlockSpec((None, None, t, HEAD_DIM), lambda hi, bi, pt_ref: (layer, bi, 0, hi))
    return pl.pallas_call(
        _prompt_attn_kernel,
        grid_spec=pltpu.PrefetchScalarGridSpec(
            num_scalar_prefetch=1,
            grid=(nh, b),
            in_specs=[pl.BlockSpec(memory_space=pltpu.SMEM), spec, kv_spec, kv_spec, spec]
                     + [page_spec(n) for n in range(pages_per_step)],
            out_specs=[spec, pl.BlockSpec((None, blocks_per_step, ch, cd), km_map)],
            scratch_shapes=[pltpu.VMEM((t, HEAD_DIM), BF16),
                            pltpu.VMEM((t, HEAD_DIM), BF16),
                            pltpu.VMEM((HEAD_DIM, t), BF16),
                            pltpu.VMEM((nbp, t), F32),
                            pltpu.VMEM((2, MOBA_BLOCK, MOBA_BLOCK), F32),
                            pltpu.VMEM((t, MOBA_BLOCK), F32),
                            pltpu.VMEM((t, MOBA_BLOCK), BF16)],
        ),
        out_shape=[jax.ShapeDtypeStruct((b, t, d_att), BF16),
                   jax.ShapeDtypeStruct((db, n_pages // PAGES_PER_BLOCK, ch, cd), F32)],
        compiler_params=_params("arbitrary", "arbitrary"),
        name="prompt_attn",
    )(page_table.reshape(-1), rel_bias, q, k_stack, v_stack, pr, *([cache] * pages_per_step))


def _conv_kernel(scw_ref, cfw_ref, cfb_ref, lng_ref, lnb_ref, psc_ref, pcf_ref,
                 sc_in_ref, sc_b_ref, sc_c_ref, z_sc_ref, cf_a_ref, cf_g_ref, z_cf_ref,
                 mix_ref, sc_state_ref, cf_state_ref, u_ext, g_ext, y_scr):
    tt, d_sc = sc_in_ref.shape
    d_cf = cf_a_ref.shape[1]

    @pl.when(pl.program_id(1) == 0)
    def _():
        g_ext[...] = jnp.zeros_like(g_ext)
        u_ext[0:SC_HALO, :] = psc_ref[...]
        g_ext[0:CF_HALO, :] = pcf_ref[...]

    u_ext[SC_HALO:SC_HALO + tt, :] = sc_c_ref[...] * sc_in_ref[...]
    g_ext[CF_HALO:CF_HALO + tt, :] = cf_a_ref[...] * jax.nn.sigmoid(cf_g_ref[...])

    sc_y = jnp.zeros((tt, d_sc), F32)
    for kk in range(SC_WIDTH):
        off = SC_HALO - (SC_WIDTH - 1) + kk
        sc_y = sc_y + scw_ref[kk:kk + 1, :] * u_ext[off:off + tt, :]
    mix_ref[:, 0:d_sc] = (sc_b_ref[...] * sc_y * _silu(z_sc_ref[...])).astype(mix_ref.dtype)

    first = CF_HALO - (CF_WIDTH - 1)
    sub = min(tt, CONV_ROWS)
    for c in range(d_cf // LANE):
        lanes = slice(c * LANE, (c + 1) * LANE)
        for r0 in range(0, tt, sub):
            acc = jnp.zeros((sub, LANE), F32)
            for rem in range(SUBLANE):
                part = None
                for base in range(0, CF_HALO + SUBLANE, SUBLANE):
                    kk = base + rem - first
                    if 0 <= kk < CF_WIDTH:
                        term = (cfw_ref[kk:kk + 1, lanes]
                                * g_ext[r0 + base:r0 + base + sub + SUBLANE, lanes])
                        part = term if part is None else part + term
                if part is not None:
                    acc = acc + part[rem:rem + sub, :]
            y_scr[r0:r0 + sub, lanes] = acc + cfb_ref[:, lanes]
    y = y_scr[...]
    mu = jnp.mean(y, axis=-1, keepdims=True)
    var = jnp.mean(jnp.square(y - mu), axis=-1, keepdims=True)
    ln = (y - mu) * lax.rsqrt(var + LN_EPS) * lng_ref[...] + lnb_ref[...]
    mix_ref[:, d_sc:d_sc + d_cf] = (_silu(ln) * _silu(z_cf_ref[...])).astype(mix_ref.dtype)

    sc_tail = u_ext[tt:tt + SC_HALO, :]
    cf_tail = g_ext[tt:tt + CF_HALO, :]
    sc_state_ref[...] = sc_tail
    cf_state_ref[...] = cf_tail
    u_ext[0:SC_HALO, :] = sc_tail
    g_ext[0:CF_HALO, :] = cf_tail


def _conv_mixers(pr, prev_sc, prev_cf, sc_w, cf_w, cf_b, ln_g, ln_b, d_att, d_sc, d_cf, tt):
    b, t, _ = pr.shape
    assert d_sc == d_cf and tt % min(tt, CONV_ROWS) == 0
    c0 = d_att // d_sc
    col = lambda n: pl.BlockSpec((None, tt, d_sc), lambda bi, ti, n=n: (bi, ti, c0 + n))
    full = lambda a: pl.BlockSpec(a.shape, lambda bi, ti: (0,) * a.ndim)
    state = lambda rows, d: pl.BlockSpec((None, rows, d), lambda bi, ti: (bi, 0, 0))
    cf_b, ln_g, ln_b = (a.reshape(1, d_cf) for a in (cf_b, ln_g, ln_b))
    ext_rows = lambda halo: -(-(halo + tt) // SUBLANE) * SUBLANE + SUBLANE
    return pl.pallas_call(
        _conv_kernel,
        grid=(b, t // tt),
        in_specs=[full(sc_w), full(cf_w), full(cf_b), full(ln_g), full(ln_b),
                  state(SC_HALO, d_sc), state(CF_HALO, d_cf)] + [col(n) for n in range(7)],
        out_specs=[pl.BlockSpec((None, tt, d_sc + d_cf), lambda bi, ti: (bi, ti, 0)),
                   state(SC_HALO, d_sc), state(CF_HALO, d_cf)],
        out_shape=[jax.ShapeDtypeStruct((b, t, d_sc + d_cf), BF16),
                   jax.ShapeDtypeStruct((b, SC_HALO, d_sc), F32),
                   jax.ShapeDtypeStruct((b, CF_HALO, d_cf), F32)],
        scratch_shapes=[pltpu.VMEM((ext_rows(SC_HALO), d_sc), F32),
                        pltpu.VMEM((ext_rows(CF_HALO), d_cf), F32),
                        pltpu.VMEM((tt, d_cf), F32)],
        compiler_params=_params("parallel", "arbitrary"),
        name="conv_mixers",
    )(sc_w, cf_w, cf_b, ln_g, ln_b, prev_sc, prev_cf, *([pr] * 7))


def _pad_history(state, halo):
    return jnp.pad(state, ((0, 0), (halo - state.shape[1], 0), (0, 0)))


def _out_proj_kernel(ma_ref, mc_ref, wa_ref, wc_ref, g_ref, x_hbm, o_ref, ss_ref, x_buf, x_sem):
    i, j = pl.program_id(0), pl.program_id(1)
    tm = o_ref.shape[0]
    tn = wa_ref.shape[1]

    def x_copy():
        return pltpu.make_async_copy(x_hbm.at[pl.ds(pl.multiple_of(i * tm, tm), tm), :], x_buf, x_sem)

    @pl.when(j == 0)
    def _():
        ss_ref[...] = jnp.zeros_like(ss_ref)
        x_copy().start()

    y = (jnp.dot(ma_ref[...].astype(BF16), wa_ref[...], preferred_element_type=F32)
         + jnp.dot(mc_ref[...].astype(BF16), wc_ref[...], preferred_element_type=F32))
    ss_ref[...] += jnp.sum(y * y, axis=-1, keepdims=True)
    o_ref[:, pl.ds(pl.multiple_of(j * tn, tn), tn)] = y

    @pl.when(j == pl.num_programs(1) - 1)
    def _():
        x_copy().wait()
        inv = lax.rsqrt(ss_ref[...] / o_ref.shape[1] + RMS_EPS)
        o_ref[...] = x_buf[...] + o_ref[...] * inv * g_ref[...]


def _out_proj(mix_att, mix_conv, w_out, layer, x, g, tm, tn):
    m, d = x.shape
    ka, kc = mix_att.shape[1], mix_conv.shape[1]
    assert ka == kc
    return pl.pallas_call(
        _out_proj_kernel,
        grid=(m // tm, d // tn),
        in_specs=[pl.BlockSpec((tm, ka), lambda i, j: (i, 0)),
                  pl.BlockSpec((tm, kc), lambda i, j: (i, 0)),
                  pl.BlockSpec((None, ka, tn), lambda i, j: (layer, 0, j)),
                  pl.BlockSpec((None, kc, tn), lambda i, j: (layer, 1, j)),
                  pl.BlockSpec((1, d), lambda i, j: (0, 0)),
                  pl.BlockSpec(memory_space=pl.ANY)],
        out_specs=pl.BlockSpec((tm, d), lambda i, j: (i, 0)),
        out_shape=jax.ShapeDtypeStruct((m, d), F32),
        scratch_shapes=[pltpu.VMEM((tm, 1), F32),
                        pltpu.VMEM((tm, d), F32),
                        pltpu.SemaphoreType.DMA(())],
        compiler_params=_params("arbitrary", "arbitrary"),
        name="out_proj",
    )(mix_att, mix_conv, w_out, w_out, g.reshape(1, d), x)


def _sample_topk_kernel(q_ref, km_ref, o_ref):
    nq = q_ref.shape[0]
    nh, nb, _ = km_ref.shape
    blk_f = lax.broadcasted_iota(I32, (nb, 1), 0).astype(F32)
    row = lax.broadcasted_iota(I32, (nq, LANE), 0)
    lane = lax.broadcasted_iota(I32, (nq, LANE), 1)
    for h in range(nh):
        cols = slice(h * HEAD_DIM, (h + 1) * HEAD_DIM)
        km = km_ref[h]
        out = jnp.zeros((nq, LANE), F32)
        for qi in range(nq):
            qv = q_ref[qi:qi + 1, cols] * (HEAD_DIM ** -0.5)
            sc = jnp.sum(km * qv, axis=1, keepdims=True)
            for r in range(MOBA_TOPK):
                best = jnp.max(sc, axis=0, keepdims=True)
                first = jnp.min(jnp.where(sc == best, blk_f, float(nb)), axis=0, keepdims=True)
                out = jnp.where((row == qi) & (lane == r), first, out)
                sc = jnp.where(blk_f == first, -jnp.inf, sc)
        o_ref[h] = out.astype(I32)


def _sample_topk(q, kmean):
    db, t, d_att = q.shape
    _, nh, nb, hd = kmean.shape
    out = pl.pallas_call(
        _sample_topk_kernel,
        grid=(db,),
        in_specs=[pl.BlockSpec((None, t, d_att), lambda bi: (bi, 0, 0)),
                  pl.BlockSpec((None, nh, nb, hd), lambda bi: (bi, 0, 0, 0))],
        out_specs=pl.BlockSpec((None, nh, t, LANE), lambda bi: (bi, 0, 0, 0)),
        out_shape=jax.ShapeDtypeStruct((db, nh, t, LANE), I32),
        compiler_params=_params("parallel"),
        name="sample_topk",
    )(q, kmean)
    return out[..., :MOBA_TOPK]


def _sample_attn_kernel(topk_ref, pt_ref, rb_ref, q_ref, kn_ref, vn_ref, z_ref, ck_hbm, cv_hbm,
                        o_ref, kbuf, vbuf, sems, *, layer, n_pages, past_len):
    n_sel = MOBA_TOPK * PAGES_PER_BLOCK
    nq = q_ref.shape[0]
    nh = pl.num_programs(1)
    h = pl.program_id(1)
    step = pl.program_id(0) * nh + h
    n_steps = pl.num_programs(0) * nh
    half = step % 2

    def chosen_block(g, qi, s):
        return topk_ref[(g * nq + qi) * MOBA_TOPK + s]

    slots = [(qi, s, p) for qi in range(nq) for s in range(MOBA_TOPK) for p in range(PAGES_PER_BLOCK)]

    def page_copies(g, buf_half, qi, s, p):
        page = pt_ref[(g // nh) * n_pages + chosen_block(g, qi, s) * PAGES_PER_BLOCK + p]
        n = qi * n_sel + s * PAGES_PER_BLOCK + p
        head = g % nh
        return (pltpu.make_async_copy(ck_hbm.at[layer, page, :, head, :], kbuf.at[buf_half, n],
                                      sems.at[0, buf_half, n]),
                pltpu.make_async_copy(cv_hbm.at[layer, page, :, head, :], vbuf.at[buf_half, n],
                                      sems.at[1, buf_half, n]))

    def start_all(g, buf_half):
        for slot in slots:
            for cp in page_copies(g, buf_half, *slot):
                cp.start()

    @pl.when(step == 0)
    def _():
        start_all(step, half)

    @pl.when(step + 1 < n_steps)
    def _():
        start_all(step + 1, 1 - half)

    for slot in slots:
        for cp in page_copies(step, half, *slot):
            cp.wait()

    prow = lax.broadcasted_iota(I32, (PAGE_SIZE, 1), 0)
    nrow = lax.broadcasted_iota(I32, (nq, 1), 0)
    far_bias = rb_ref[h, N_BUCKETS - 1]
    for qi in range(nq):
        q_pos = past_len + qi
        qv = q_ref[qi:qi + 1, :] * (HEAD_DIM ** -0.5)
        logits = []
        for s in range(MOBA_TOPK):
            blk = chosen_block(step, qi, s)
            for p in range(PAGES_PER_BLOCK):
                kp = kbuf[half, qi * n_sel + s * PAGES_PER_BLOCK + p]
                lg = jnp.sum(kp * qv, axis=1, keepdims=True)
                first_pos = blk * MOBA_BLOCK + p * PAGE_SIZE
                dist = q_pos - first_pos - prow
                near = q_pos - (first_pos + PAGE_SIZE - 1) < MAX_DISTANCE
                bias = lax.cond(near,
                                lambda dist=dist: _t5_bias(dist, rb_ref, h),
                                lambda: jnp.full((PAGE_SIZE, 1), far_bias, F32))
                logits.append(lg + bias)
        own = jnp.sum(kn_ref[...] * qv, axis=1, keepdims=True) + _t5_bias(qi - nrow, rb_ref, h)
        own = jnp.where(nrow <= qi, own, NEG_INF)
        m = jnp.max(own, axis=0, keepdims=True)
        for lg in logits:
            m = jnp.maximum(m, jnp.max(lg, axis=0, keepdims=True))
        p_own = jnp.exp(own - m)
        l = jnp.sum(p_own, axis=0, keepdims=True)
        acc = jnp.sum(p_own * vn_ref[...], axis=0, keepdims=True)
        for n, lg in enumerate(logits):
            pn = jnp.exp(lg - m)
            l = l + jnp.sum(pn, axis=0, keepdims=True)
            acc = acc + jnp.sum(pn * vbuf[half, qi * n_sel + n], axis=0, keepdims=True)
        o_ref[qi:qi + 1, :] = acc / l * _silu(z_ref[qi:qi + 1, :])


def _sample_attention(q, k_stack, v_stack, pr, cache_k, cache_v, layer, page_table, topk, rel_bias):
    db, t, d_att = q.shape
    nh = d_att // HEAD_DIM
    n_pages = page_table.shape[1]
    past_len = n_pages * PAGE_SIZE
    assert past_len % MOBA_BLOCK == 0 and t <= MOBA_BLOCK
    n_buf = t * MOBA_TOPK * PAGES_PER_BLOCK
    head = pl.BlockSpec((None, t, HEAD_DIM), lambda bi, hi, topk_ref, pt_ref: (bi, 0, hi))
    new_rows = pl.BlockSpec((None, None, t, HEAD_DIM),
                            lambda bi, hi, topk_ref, pt_ref: (layer, bi, 0, hi))
    hbm = pl.BlockSpec(memory_space=pl.ANY)
    return pl.pallas_call(
        functools.partial(_sample_attn_kernel, layer=layer, n_pages=n_pages, past_len=past_len),
        grid_spec=pltpu.PrefetchScalarGridSpec(
            num_scalar_prefetch=2,
            grid=(db, nh),
            in_specs=[pl.BlockSpec(memory_space=pltpu.SMEM), head, new_rows, new_rows, head, hbm, hbm],
            out_specs=head,
            scratch_shapes=[pltpu.VMEM((2, n_buf, PAGE_SIZE, HEAD_DIM), F32),
                            pltpu.VMEM((2, n_buf, PAGE_SIZE, HEAD_DIM), F32),
                            pltpu.SemaphoreType.DMA((2, 2, n_buf))],
        ),
        out_shape=jax.ShapeDtypeStruct((db, t, d_att), F32),
        compiler_params=_params("arbitrary", "arbitrary"),
        name="sample_attn",
    )(topk.reshape(-1), page_table.reshape(-1), rel_bias, q, k_stack, v_stack, pr, cache_k, cache_v)


def _proj_tiles(m):
    if m % 2048 == 0:
        return (2048, 256), 512
    return (m, 1024), m


def _layer(x, prev_sc, prev_cf, attend, layer, kv_stack, w_in, w_out, g_pre, g_post, sc_w, cf_w,
           cf_b, ln_g, ln_b, tt):
    b, t, d = x.shape
    d_sc = sc_w.shape[1]
    d_cf = cf_w.shape[1]
    d_in = w_in.shape[2]
    d_att = (d_in - 4 * d_sc - 3 * d_cf) // 4
    m = b * t
    (tm_in, tn_in), tm_out = _proj_tiles(m)
    x2 = x.reshape(m, d)
    h = _rmsnorm(x2, g_pre, min(tm_out, 256))
    q = _matmul_cols(h, w_in, layer, 0, d_att, tm_in, tn_in).reshape(b, t, d_att)
    k_stack, v_stack = (
        _matmul_cols(h, w_in, layer, (n + 1) * d_att, d_att, tm_in, tn_in, stack=kv_stack and kv_stack[n])
        for n in range(2))
    pr = _matmul_cols(h, w_in, layer, 3 * d_att, d_in - 3 * d_att, tm_in, tn_in).reshape(b, t, -1)
    depth = w_in.shape[0]
    mix_att = attend(q, k_stack.reshape(depth, b, t, d_att), v_stack.reshape(depth, b, t, d_att), pr)
    mix_conv, sc_state, cf_state = _conv_mixers(pr, prev_sc, prev_cf, sc_w, cf_w, cf_b, ln_g, ln_b,
                                                d_att, d_sc, d_cf, tt)
    y = _out_proj(mix_att.reshape(m, d_att), mix_conv.reshape(m, d_sc + d_cf), w_out, layer, x2,
                  g_post, tm_out, 512)
    return (y.reshape(b, t, d), (k_stack, v_stack),
            sc_state[:, SC_HALO - (SC_WIDTH - 1):], cf_state[:, CF_HALO - (CF_WIDTH - 1):])


def kernel(x_prompt, x_sample, cache_k, cache_v, state_sc, state_cf, page_table, w_in, w_out,
           norm_pre, norm_post, sc_w, cf_w, cf_b, cf_ln_g, cf_ln_b, rel_bias):
    depth = w_in.shape[0]
    b, t, _ = x_prompt.shape
    db, dt, _ = x_sample.shape
    d_sc, d_cf = sc_w.shape[2], cf_w.shape[2]
    nh, hd = cache_k.shape[3], cache_k.shape[4]
    w_out_b = w_out.astype(BF16)
    zero_sc = jnp.zeros((b, SC_HALO, d_sc), F32)
    zero_cf = jnp.zeros((b, CF_HALO, d_cf), F32)

    hp, hs = x_prompt, x_sample
    kv_p, kv_s = (), ()
    outs = [[] for _ in range(4)]
    for l in range(depth):
        lw = (w_in, w_out_b, norm_pre[l], norm_post[l], sc_w[l], cf_w[l], cf_b[l],
              cf_ln_g[l], cf_ln_b[l])
        kmean = []

        def attend_prompt(q, k_stack, v_stack, pr, l=l, kmean=kmean):
            mix, km = _prompt_attention(q, k_stack, v_stack, pr, rel_bias, l, cache_k, page_table)
            kmean.append(km)
            return mix

        def attend_sample(q, k_stack, v_stack, pr, l=l, kmean=kmean):
            topk = _sample_topk(q, kmean[0].transpose(0, 2, 1, 3))
            return _sample_attention(q, k_stack, v_stack, pr, cache_k, cache_v, l, page_table, topk,
                                     rel_bias)

        hp, kv_p, scp, cfp = _layer(hp, zero_sc, zero_cf, attend_prompt, l, kv_p, *lw, tt=256)
        hs, kv_s, scs, cfs = _layer(hs, _pad_history(state_sc[l], SC_HALO),
                                    _pad_history(state_cf[l], CF_HALO), attend_sample, l, kv_s, *lw,
                                    tt=dt)
        for lst, val in zip(outs, (scp, scs, cfp, cfs)):
            lst.append(val)

    scp, scs, cfp, cfs = (jnp.stack(o) for o in outs)
    return (hp, hs,
            kv_p[0].reshape(depth, b, t, nh, hd), kv_p[1].reshape(depth, b, t, nh, hd),
            kv_s[0].reshape(depth, db, dt, nh, hd), kv_s[1].reshape(depth, db, dt, nh, hd),
            scp, scs, cfp, cfs)
```

```python
import functools
import math

import jax
import jax.numpy as jnp
from jax import lax
from jax.experimental import pallas as pl
from jax.experimental.pallas import tpu as pltpu

F32 = jnp.float32
BF16 = jnp.bfloat16
I32 = jnp.int32

HEAD_DIM = 128
PAGE_SIZE = 128
SC_WIDTH = 3
CF_WIDTH = 31
MOBA_BLOCK = 256
MOBA_TOPK = 3
N_BUCKETS = 32
MAX_DISTANCE = 128
RMS_EPS = 1e-6
LN_EPS = 1e-5
NEG_INF = -1e30

PAGES_PER_BLOCK = MOBA_BLOCK // PAGE_SIZE
SUBLANE = 8
BF16_SUBLANE = 16
LANE = 128
SC_HALO = -(-(SC_WIDTH - 1) // SUBLANE) * SUBLANE
CF_HALO = -(-(CF_WIDTH - 1) // SUBLANE) * SUBLANE
CONV_ROWS = 128

VMEM_LIMIT = 56 * 1024 * 1024


def _params(*sem):
    return pltpu.CompilerParams(dimension_semantics=sem, vmem_limit_bytes=VMEM_LIMIT)


def _silu(x):
    return x * jax.nn.sigmoid(x)


def _rmsnorm_kernel(x_ref, g_ref, o_ref):
    x = x_ref[...]
    ms = jnp.mean(x * x, axis=-1, keepdims=True)
    o_ref[...] = (x * lax.rsqrt(ms + RMS_EPS) * g_ref[...]).astype(o_ref.dtype)


def _rmsnorm(x, g, tm):
    m, d = x.shape
    return pl.pallas_call(
        _rmsnorm_kernel,
        grid=(m // tm,),
        in_specs=[pl.BlockSpec((tm, d), lambda i: (i, 0)),
                  pl.BlockSpec((1, d), lambda i: (0, 0))],
        out_specs=pl.BlockSpec((tm, d), lambda i: (i, 0)),
        out_shape=jax.ShapeDtypeStruct((m, d), BF16),
        compiler_params=_params("parallel"),
        name="rmsnorm_pre",
    )(x, g.reshape(1, d))


def _matmul_kernel(a_ref, w_ref, *rest):
    o_ref = rest[-1]
    o_ref[...] = jnp.dot(a_ref[...], w_ref[...].astype(BF16), preferred_element_type=F32)


def _matmul_cols(a, w, layer, col0, ncols, tm, tn, stack=None):
    m, k = a.shape
    c0 = col0 // tn
    in_specs = [pl.BlockSpec((tm, k), lambda i, j: (i, 0)),
                pl.BlockSpec((None, k, tn), lambda i, j: (layer, 0, c0 + j))]
    args = [a, w]
    aliases = {}
    if stack is None:
        out_spec = pl.BlockSpec((tm, tn), lambda i, j: (i, j))
        out_shape = jax.ShapeDtypeStruct((m, ncols), F32)
    else:
        out_spec = pl.BlockSpec((None, tm, tn), lambda i, j: (layer, i, j))
        out_shape = jax.ShapeDtypeStruct(stack.shape, F32)
        in_specs.append(pl.BlockSpec(memory_space=pl.ANY))
        args.append(stack)
        aliases = {2: 0}
    return pl.pallas_call(
        _matmul_kernel,
        grid=(m // tm, ncols // tn),
        in_specs=in_specs,
        out_specs=out_spec,
        out_shape=out_shape,
        input_output_aliases=aliases,
        compiler_params=_params("parallel", "parallel"),
        name="in_proj",
    )(*args)


def _t5_bias(dist, rb_ref, h):
    max_exact = N_BUCKETS // 2
    d = jnp.maximum(dist, 0)
    ratio = jnp.log(jnp.maximum(d, 1).astype(F32) / max_exact) / math.log(MAX_DISTANCE / max_exact)
    large = max_exact + (ratio * (N_BUCKETS - max_exact)).astype(I32)
    large = jnp.minimum(large, N_BUCKETS - 1)
    bucket = jnp.where(d < max_exact, d, large)
    bias = jnp.zeros(dist.shape, F32)
    for b in range(N_BUCKETS):
        bias = jnp.where(bucket == b, rb_ref[h, b], bias)
    return bias


def _split_bf16(x):
    hi = x.astype(BF16)
    lo = (x - hi.astype(F32)).astype(BF16)
    return hi, lo


_NT = (((1,), (1,)), ((), ()))


def _block_means(page_refs, o_ref):
    for n in range(o_ref.shape[0]):
        total = jnp.zeros(o_ref.shape[1:], F32)
        for p in range(PAGES_PER_BLOCK):
            total = total + jnp.sum(page_refs[n * PAGES_PER_BLOCK + p][...], axis=0)
        o_ref[n] = total * (1.0 / MOBA_BLOCK)


def _prompt_attn_kernel(pt_ref, rb_ref, q_ref, k_ref, v_ref, z_ref, *refs):
    qb_ref, kb_ref, vt_ref, sel_ref, bias_ref, s_ref, p_ref = refs[-7:]
    o_ref, km_ref = refs[-9:-7]
    _block_means(refs[:-9], km_ref)

    h, b = pl.program_id(0), pl.program_id(1)
    t = q_ref.shape[0]
    nb = t // MOBA_BLOCK
    blk = MOBA_BLOCK
    key = lax.broadcasted_iota(I32, (blk, blk), 0)
    qry = lax.broadcasted_iota(I32, (blk, blk), 1)

    @pl.when(b == 0)
    def _():
        bias_ref[0] = _t5_bias(qry - key, rb_ref, h)
        bias_ref[1] = _t5_bias(qry - key + blk, rb_ref, h)
    far_bias = rb_ref[h, N_BUCKETS - 1]

    kf = k_ref[...]
    qs = q_ref[...] * (HEAD_DIM ** -0.5)
    qh, ql = _split_bf16(qs)
    qb_ref[...] = qh
    kb_ref[...] = kf.astype(BF16)
    for j in range(nb):
        cols = slice(j * blk, (j + 1) * blk)
        vt_ref[:, cols] = v_ref[cols, :].T.astype(BF16)

    kmean = jnp.mean(kf.reshape(nb, blk, HEAD_DIM), axis=1)
    nbp = sel_ref.shape[0]
    kmean = jnp.concatenate([kmean, jnp.zeros((nbp - nb, HEAD_DIM), F32)], axis=0)
    kh, kl = _split_bf16(kmean)
    score = (lax.dot_general(kh, qh, _NT, preferred_element_type=F32)
             + lax.dot_general(kl, qh, _NT, preferred_element_type=F32)
             + lax.dot_general(kh, ql, _NT, preferred_element_type=F32))
    blk_id = lax.broadcasted_iota(I32, (nbp, t), 0)
    own = lax.broadcasted_iota(I32, (nbp, t), 1) // blk
    past = blk_id < own
    blk_f = blk_id.astype(F32)
    sc = jnp.where(past, score, NEG_INF)
    picked = jnp.zeros((nbp, t), F32)
    for _ in range(min(MOBA_TOPK, nb)):
        best = jnp.max(sc, axis=0, keepdims=True)
        first = jnp.min(jnp.where(sc == best, blk_f, float(nbp)), axis=0, keepdims=True)
        hit = blk_f == first
        picked = jnp.where(hit, 1.0, picked)
        sc = jnp.where(hit, -jnp.inf, sc)
    sel_ref[...] = jnp.where(past, picked, 0.0)

    for i in range(nb):
        rows = slice(i * blk, (i + 1) * blk)
        qi = qb_ref[rows, :]
        m = None
        for j in range(i + 1):
            cols = slice(j * blk, (j + 1) * blk)
            s = lax.dot_general(kb_ref[cols, :], qi, _NT, preferred_element_type=F32)
            if j == i:
                s = jnp.where(key <= qry, s + bias_ref[0], NEG_INF)
            else:
                s = s + (bias_ref[1] if j == i - 1 else far_bias)
                s = jnp.where(sel_ref[j:j + 1, rows] > 0.0, s, NEG_INF)
            s_ref[cols, :] = s
            mj = jnp.max(s, axis=0, keepdims=True)
            m = mj if m is None else jnp.maximum(m, mj)
        l = jnp.zeros((1, blk), F32)
        for j in range(i + 1):
            cols = slice(j * blk, (j + 1) * blk)
            p = jnp.exp(s_ref[cols, :] - m)
            l = l + jnp.sum(p, axis=0, keepdims=True)
            p_ref[cols, :] = p.astype(BF16)
        n = (i + 1) * blk
        acc = jnp.dot(vt_ref[:, 0:n], p_ref[0:n, :], preferred_element_type=F32)
        o_ref[rows, :] = ((acc / l).T * _silu(z_ref[rows, :])).astype(o_ref.dtype)


def _prompt_attention(q, k_stack, v_stack, pr, rel_bias, layer, cache, page_table):
    b, t, d_att = q.shape
    nh = d_att // HEAD_DIM
    nb = t // MOBA_BLOCK
    nbp = -(-nb // BF16_SUBLANE) * BF16_SUBLANE
    _, _, page, ch, cd = cache.shape
    db, n_pages = page_table.shape
    pages_per_step = db * n_pages // (nh * b)
    assert pages_per_step * nh * b == db * n_pages and n_pages % pages_per_step == 0
    assert pages_per_step % PAGES_PER_BLOCK == 0
    steps_per_seq = n_pages // pages_per_step
    blocks_per_step = pages_per_step // PAGES_PER_BLOCK

    def page_spec(n):
        def imap(hi, bi, pt_ref):
            return (layer, pt_ref[(hi * b + bi) * pages_per_step + n], 0, 0, 0)
        return pl.BlockSpec((None, None, page, ch, cd), imap)

    def km_map(hi, bi, pt_ref):
        step = hi * b + bi
        return (step // steps_per_seq, step % steps_per_seq, 0, 0)

    spec = pl.BlockSpec((None, t, HEAD_DIM), lambda hi, bi, pt_ref: (bi, 0, hi))
    kv_spec = pl.BlockSpec((None, None, t, HEAD_DIM), lambda hi, bi, pt_ref: (layer, bi, 0, hi))
    return pl.pallas_call(
        _prompt_attn_kernel,
        grid_spec=pltpu.PrefetchScalarGridSpec(
            num_scalar_prefetch=1,
            grid=(nh, b),
            in_specs=[pl.BlockSpec(memory_space=pltpu.SMEM), spec, kv_spec, kv_spec, spec]
                     + [page_spec(n) for n in range(pages_per_step)],
            out_specs=[spec, pl.BlockSpec((None, blocks_per_step, ch, cd), km_map)],
            scratch_shapes=[pltpu.VMEM((t, HEAD_DIM), BF16),
                            pltpu.VMEM((t, HEAD_DIM), BF16),
                            pltpu.VMEM((HEAD_DIM, t), BF16),
                            pltpu.VMEM((nbp, t), F32),
                            pltpu.VMEM((2, MOBA_BLOCK, MOBA_BLOCK), F32),
                            pltpu.VMEM((t, MOBA_BLOCK), F32),
                            pltpu.VMEM((t, MOBA_BLOCK), BF16)],
        ),
        out_shape=[jax.ShapeDtypeStruct((b, t, d_att), BF16),
                   jax.ShapeDtypeStruct((db, n_pages // PAGES_PER_BLOCK, ch, cd), F32)],
        compiler_params=_params("arbitrary", "arbitrary"),
        name="prompt_attn",
    )(page_table.reshape(-1), rel_bias, q, k_stack, v_stack, pr, *([cache] * pages_per_step))


def _conv_kernel(scw_ref, cfw_ref, cfb_ref, lng_ref, lnb_ref, psc_ref, pcf_ref,
                 sc_in_ref, sc_b_ref, sc_c_ref, z_sc_ref, cf_a_ref, cf_g_ref, z_cf_ref,
                 mix_ref, sc_state_ref, cf_state_ref, u_ext, g_ext, y_scr):
    tt, d_sc = sc_in_ref.shape
    d_cf = cf_a_ref.shape[1]

    @pl.when(pl.program_id(1) == 0)
    def _():
        g_ext[...] = jnp.zeros_like(g_ext)
        u_ext[0:SC_HALO, :] = psc_ref[...]
        g_ext[0:CF_HALO, :] = pcf_ref[...]

    u_ext[SC_HALO:SC_HALO + tt, :] = sc_c_ref[...] * sc_in_ref[...]
    g_ext[CF_HALO:CF_HALO + tt, :] = cf_a_ref[...] * jax.nn.sigmoid(cf_g_ref[...])

    sc_y = jnp.zeros((tt, d_sc), F32)
    for kk in range(SC_WIDTH):
        off = SC_HALO - (SC_WIDTH - 1) + kk
        sc_y = sc_y + scw_ref[kk:kk + 1, :] * u_ext[off:off + tt, :]
    mix_ref[:, 0:d_sc] = (sc_b_ref[...] * sc_y * _silu(z_sc_ref[...])).astype(mix_ref.dtype)

    first = CF_HALO - (CF_WIDTH - 1)
    sub = min(tt, CONV_ROWS)
    for c in range(d_cf // LANE):
        lanes = slice(c * LANE, (c + 1) * LANE)
        for r0 in range(0, tt, sub):
            acc = jnp.zeros((sub, LANE), F32)
            for rem in range(SUBLANE):
                part = None
                for base in range(0, CF_HALO + SUBLANE, SUBLANE):
                    kk = base + rem - first
                    if 0 <= kk < CF_WIDTH:
                        term = (cfw_ref[kk:kk + 1, lanes]
                                * g_ext[r0 + base:r0 + base + sub + SUBLANE, lanes])
                        part = term if part is None else part + term
                if part is not None:
                    acc = acc + part[rem:rem + sub, :]
            y_scr[r0:r0 + sub, lanes] = acc + cfb_ref[:, lanes]
    y = y_scr[...]
    mu = jnp.mean(y, axis=-1, keepdims=True)
    var = jnp.mean(jnp.square(y - mu), axis=-1, keepdims=True)
    ln = (y - mu) * lax.rsqrt(var + LN_EPS) * lng_ref[...] + lnb_ref[...]
    mix_ref[:, d_sc:d_sc + d_cf] = (_silu(ln) * _silu(z_cf_ref[...])).astype(mix_ref.dtype)

    sc_tail = u_ext[tt:tt + SC_HALO, :]
    cf_tail = g_ext[tt:tt + CF_HALO, :]
    sc_state_ref[...] = sc_tail
    cf_state_ref[...] = cf_tail
    u_ext[0:SC_HALO, :] = sc_tail
    g_ext[0:CF_HALO, :] = cf_tail


def _conv_mixers(pr, prev_sc, prev_cf, sc_w, cf_w, cf_b, ln_g, ln_b, d_att, d_sc, d_cf, tt):
    b, t, _ = pr.shape
    assert d_sc == d_cf and tt % min(tt, CONV_ROWS) == 0
    c0 = d_att // d_sc
    col = lambda n: pl.BlockSpec((None, tt, d_sc), lambda bi, ti, n=n: (bi, ti, c0 + n))
    full = lambda a: pl.BlockSpec(a.shape, lambda bi, ti: (0,) * a.ndim)
    state = lambda rows, d: pl.BlockSpec((None, rows, d), lambda bi, ti: (bi, 0, 0))
    cf_b, ln_g, ln_b = (a.reshape(1, d_cf) for a in (cf_b, ln_g, ln_b))
    ext_rows = lambda halo: -(-(halo + tt) // SUBLANE) * SUBLANE + SUBLANE
    return pl.pallas_call(
        _conv_kernel,
        grid=(b, t // tt),
        in_specs=[full(sc_w), full(cf_w), full(cf_b), full(ln_g), full(ln_b),
                  state(SC_HALO, d_sc), state(CF_HALO, d_cf)] + [col(n) for n in range(7)],
        out_specs=[pl.BlockSpec((None, tt, d_sc + d_cf), lambda bi, ti: (bi, ti, 0)),
                   state(SC_HALO, d_sc), state(CF_HALO, d_cf)],
        out_shape=[jax.ShapeDtypeStruct((b, t, d_sc + d_cf), BF16),
                   jax.ShapeDtypeStruct((b, SC_HALO, d_sc), F32),
                   jax.ShapeDtypeStruct((b, CF_HALO, d_cf), F32)],
        scratch_shapes=[pltpu.VMEM((ext_rows(SC_HALO), d_sc), F32),
                        pltpu.VMEM((ext_rows(CF_HALO), d_cf), F32),
                        pltpu.VMEM((tt, d_cf), F32)],
        compiler_params=_params("parallel", "arbitrary"),
        name="conv_mixers",
    )(sc_w, cf_w, cf_b, ln_g, ln_b, prev_sc, prev_cf, *([pr] * 7))


def _pad_history(state, halo):
    return jnp.pad(state, ((0, 0), (halo - state.shape[1], 0), (0, 0)))


def _sample_topk_kernel(q_ref, km_ref, o_ref):
    nq = q_ref.shape[0]
    nh, nb, _ = km_ref.shape
    blk_f = lax.broadcasted_iota(I32, (nb, 1), 0).astype(F32)
    row = lax.broadcasted_iota(I32, (nq, LANE), 0)
    lane = lax.broadcasted_iota(I32, (nq, LANE), 1)
    for h in range(nh):
        cols = slice(h * HEAD_DIM, (h + 1) * HEAD_DIM)
        km = km_ref[h]
        out = jnp.zeros((nq, LANE), F32)
        for qi in range(nq):
            qv = q_ref[qi:qi + 1, cols] * (HEAD_DIM ** -0.5)
            sc = jnp.sum(km * qv, axis=1, keepdims=True)
            for r in range(MOBA_TOPK):
                best = jnp.max(sc, axis=0, keepdims=True)
                first = jnp.min(jnp.where(sc == best, blk_f, float(nb)), axis=0, keepdims=True)
                out = jnp.where((row == qi) & (lane == r), first, out)
                sc = jnp.where(blk_f == first, -jnp.inf, sc)
        o_ref[h] = out.astype(I32)


def _sample_topk(q, kmean):
    db, t, d_att = q.shape
    _, nh, nb, hd = kmean.shape
    out = pl.pallas_call(
        _sample_topk_kernel,
        grid=(db,),
        in_specs=[pl.BlockSpec((None, t, d_att), lambda bi: (bi, 0, 0)),
                  pl.BlockSpec((None, nh, nb, hd), lambda bi: (bi, 0, 0, 0))],
        out_specs=pl.BlockSpec((None, nh, t, LANE), lambda bi: (bi, 0, 0, 0)),
        out_shape=jax.ShapeDtypeStruct((db, nh, t, LANE), I32),
        compiler_params=_params("parallel"),
        name="sample_topk",
    )(q, kmean)
    return out[..., :MOBA_TOPK]


class _SampleGather:
    def __init__(self, topk_ref, pt_ref, ck_hbm, cv_hbm, kbuf, vbuf, sems, *, layer, n_pages, nh, nq):
        self.topk_ref, self.pt_ref = topk_ref, pt_ref
        self.src = (ck_hbm, cv_hbm)
        self.dst = (kbuf, vbuf)
        self.sems = sems
        self.layer, self.n_pages, self.nh, self.nq = layer, n_pages, nh, nq
        self.n_sel = MOBA_TOPK * PAGES_PER_BLOCK
        self.slots = [(qi, s, p) for qi in range(nq) for s in range(MOBA_TOPK)
                      for p in range(PAGES_PER_BLOCK)]

    def chosen_block(self, g, qi, s):
        return self.topk_ref[(g * self.nq + qi) * MOBA_TOPK + s]

    def slot_index(self, qi, s, p):
        return qi * self.n_sel + s * PAGES_PER_BLOCK + p

    def _copies(self, g, half, qi, s, p):
        page = self.pt_ref[(g // self.nh) * self.n_pages
                           + self.chosen_block(g, qi, s) * PAGES_PER_BLOCK + p]
        n = self.slot_index(qi, s, p)
        head = g % self.nh
        return [pltpu.make_async_copy(src.at[self.layer, page, :, head, :], dst.at[half, n],
                                      self.sems.at[which, half, n])
                for which, (src, dst) in enumerate(zip(self.src, self.dst))]

    def start(self, g, half):
        for slot in self.slots:
            for cp in self._copies(g, half, *slot):
                cp.start()

    def wait(self, g, half):
        for slot in self.slots:
            for cp in self._copies(g, half, *slot):
                cp.wait()


def _sample_attend(gather, step, rb_ref, q_ref, kn_ref, vn_ref, z_ref, o_ref, *, past_len):
    nq = gather.nq
    h = step % gather.nh
    half = step % 2
    kbuf, vbuf = gather.dst
    prow = lax.broadcasted_iota(I32, (PAGE_SIZE, 1), 0)
    nrow = lax.broadcasted_iota(I32, (nq, 1), 0)
    far_bias = rb_ref[h, N_BUCKETS - 1]
    for qi in range(nq):
        q_pos = past_len + qi
        qv = q_ref[qi:qi + 1, :] * (HEAD_DIM ** -0.5)
        logits = []
        for s in range(MOBA_TOPK):
            blk = gather.chosen_block(step, qi, s)
            for p in range(PAGES_PER_BLOCK):
                kp = kbuf[half, gather.slot_index(qi, s, p)]
                lg = jnp.sum(kp * qv, axis=1, keepdims=True)
                first_pos = blk * MOBA_BLOCK + p * PAGE_SIZE
                dist = q_pos - first_pos - prow
                near = q_pos - (first_pos + PAGE_SIZE - 1) < MAX_DISTANCE
                bias = lax.cond(near,
                                lambda dist=dist: _t5_bias(dist, rb_ref, h),
                                lambda: jnp.full((PAGE_SIZE, 1), far_bias, F32))
                logits.append((lg + bias, gather.slot_index(qi, s, p)))
        own = jnp.sum(kn_ref[...] * qv, axis=1, keepdims=True) + _t5_bias(qi - nrow, rb_ref, h)
        own = jnp.where(nrow <= qi, own, NEG_INF)
        m = jnp.max(own, axis=0, keepdims=True)
        for lg, _ in logits:
            m = jnp.maximum(m, jnp.max(lg, axis=0, keepdims=True))
        p_own = jnp.exp(own - m)
        l = jnp.sum(p_own, axis=0, keepdims=True)
        acc = jnp.sum(p_own * vn_ref[...], axis=0, keepdims=True)
        for lg, n in logits:
            pn = jnp.exp(lg - m)
            l = l + jnp.sum(pn, axis=0, keepdims=True)
            acc = acc + jnp.sum(pn * vbuf[half, n], axis=0, keepdims=True)
        o_ref[qi:qi + 1, :] = acc / l * _silu(z_ref[qi:qi + 1, :])


def _out_proj_step(ma_ref, mc_ref, wa_ref, wc_ref, g_ref, x_hbm, o_ref, ss_ref, x_buf, x_sem):
    i, j = pl.program_id(0), pl.program_id(1)
    tm = o_ref.shape[0]
    tn = wa_ref.shape[1]

    def x_copy():
        return pltpu.make_async_copy(x_hbm.at[pl.ds(pl.multiple_of(i * tm, tm), tm), :], x_buf, x_sem)

    @pl.when(j == 0)
    def _():
        ss_ref[...] = jnp.zeros_like(ss_ref)
        x_copy().start()

    y = (jnp.dot(ma_ref[...].astype(BF16), wa_ref[...], preferred_element_type=F32)
         + jnp.dot(mc_ref[...].astype(BF16), wc_ref[...], preferred_element_type=F32))
    ss_ref[...] += jnp.sum(y * y, axis=-1, keepdims=True)
    o_ref[:, pl.ds(pl.multiple_of(j * tn, tn), tn)] = y

    @pl.when(j == pl.num_programs(1) - 1)
    def _():
        x_copy().wait()
        inv = lax.rsqrt(ss_ref[...] / o_ref.shape[1] + RMS_EPS)
        o_ref[...] = x_buf[...] + o_ref[...] * inv * g_ref[...]


def _out_proj_kernel(*refs):
    _out_proj_step(*refs)


def _out_proj_hosting_kernel(topk_ref, pt_ref, ma_ref, mc_ref, wa_ref, wc_ref, g_ref, x_hbm,
                             rb_ref, q_ref, kn_ref, vn_ref, z_ref, ck_hbm, cv_hbm,
                             o_ref, mix_ref, ss_ref, x_buf, x_sem, kbuf, vbuf, sems,
                             *, layer, n_pages, nh):
    step = pl.program_id(0) * pl.num_programs(1) + pl.program_id(1)
    n_steps = pl.num_programs(0) * pl.num_programs(1)
    half = step % 2
    gather = _SampleGather(topk_ref, pt_ref, ck_hbm, cv_hbm, kbuf, vbuf, sems,
                           layer=layer, n_pages=n_pages, nh=nh, nq=q_ref.shape[0])

    @pl.when(step == 0)
    def _():
        gather.start(step, half)

    @pl.when(step + 1 < n_steps)
    def _():
        gather.start(step + 1, 1 - half)

    _out_proj_step(ma_ref, mc_ref, wa_ref, wc_ref, g_ref, x_hbm, o_ref, ss_ref, x_buf, x_sem)

    gather.wait(step, half)
    _sample_attend(gather, step, rb_ref, q_ref, kn_ref, vn_ref, z_ref, mix_ref,
                   past_len=n_pages * PAGE_SIZE)


def _out_proj(mix_att, mix_conv, w_out, layer, x, g, tm, tn, sample=None):
    m, d = x.shape
    ka, kc = mix_att.shape[1], mix_conv.shape[1]
    assert ka == kc
    grid = (m // tm, d // tn)
    in_specs = [pl.BlockSpec((tm, ka), lambda i, j, *_: (i, 0)),
                pl.BlockSpec((tm, kc), lambda i, j, *_: (i, 0)),
                pl.BlockSpec((None, ka, tn), lambda i, j, *_: (layer, 0, j)),
                pl.BlockSpec((None, kc, tn), lambda i, j, *_: (layer, 1, j)),
                pl.BlockSpec((1, d), lambda i, j, *_: (0, 0)),
                pl.BlockSpec(memory_space=pl.ANY)]
    out_spec = pl.BlockSpec((tm, d), lambda i, j, *_: (i, 0))
    out_shape = jax.ShapeDtypeStruct((m, d), F32)
    scratch = [pltpu.VMEM((tm, 1), F32), pltpu.VMEM((tm, d), F32), pltpu.SemaphoreType.DMA(())]
    args = (mix_att, mix_conv, w_out, w_out, g.reshape(1, d), x)
    if sample is None:
        return pl.pallas_call(
            _out_proj_kernel, grid=grid, in_specs=in_specs, out_specs=out_spec, out_shape=out_shape,
            scratch_shapes=scratch, compiler_params=_params("arbitrary", "arbitrary"),
            name="out_proj",
        )(*args)

    q, k_stack, v_stack, pr, cache_k, cache_v, page_table, topk, rel_bias = sample
    db, t, d_att = q.shape
    nh = d_att // HEAD_DIM
    n_pages = page_table.shape[1]
    assert (n_pages * PAGE_SIZE) % MOBA_BLOCK == 0 and t <= MOBA_BLOCK
    assert grid[0] * grid[1] == db * nh, "one sample (batch, head) per grid step"
    n_buf = t * MOBA_TOPK * PAGES_PER_BLOCK

    def head_map(i, j, *_):
        step = i * grid[1] + j
        return (step // nh, 0, step % nh)

    head = pl.BlockSpec((None, t, HEAD_DIM), head_map)
    new_rows = pl.BlockSpec((None, None, t, HEAD_DIM), lambda i, j, *_: (layer,) + head_map(i, j))
    hbm = pl.BlockSpec(memory_space=pl.ANY)
    return pl.pallas_call(
        functools.partial(_out_proj_hosting_kernel, layer=layer, n_pages=n_pages, nh=nh),
        grid_spec=pltpu.PrefetchScalarGridSpec(
            num_scalar_prefetch=2,
            grid=grid,
            in_specs=in_specs + [pl.BlockSpec(memory_space=pltpu.SMEM), head, new_rows, new_rows,
                                 head, hbm, hbm],
            out_specs=[out_spec, head],
            scratch_shapes=scratch + [pltpu.VMEM((2, n_buf, PAGE_SIZE, HEAD_DIM), F32),
                                      pltpu.VMEM((2, n_buf, PAGE_SIZE, HEAD_DIM), F32),
                                      pltpu.SemaphoreType.DMA((2, 2, n_buf))],
        ),
        out_shape=[out_shape, jax.ShapeDtypeStruct((db, t, d_att), F32)],
        compiler_params=_params("arbitrary", "arbitrary"),
        name="out_proj_sample_attn",
    )(topk.reshape(-1), page_table.reshape(-1), *args, rel_bias, q, k_stack, v_stack, pr,
      cache_k, cache_v)


def _proj_tiles(m):
    if m % 2048 == 0:
        return (2048, 256), 512
    return (m, 1024), m


def _project_in(x, layer, kv_stack, w_in, g_pre, d_att):
    b, t, d = x.shape
    m = b * t
    (tm, tn), tm_norm = _proj_tiles(m)
    h = _rmsnorm(x.reshape(m, d), g_pre, min(tm_norm, 256))
    q = _matmul_cols(h, w_in, layer, 0, d_att, tm, tn).reshape(b, t, d_att)
    k_stack, v_stack = (
        _matmul_cols(h, w_in, layer, (n + 1) * d_att, d_att, tm, tn,
                     stack=kv_stack[n].reshape(-1, m, d_att)).reshape(-1, b, t, d_att)
        for n in range(2))
    pr = _matmul_cols(h, w_in, layer, 3 * d_att, w_in.shape[2] - 3 * d_att, tm, tn).reshape(b, t, -1)
    return q, (k_stack, v_stack), pr


def kernel(x_prompt, x_sample, cache_k, cache_v, state_sc, state_cf, page_table, w_in, w_out,
           norm_pre, norm_post, sc_w, cf_w, cf_b, cf_ln_g, cf_ln_b, rel_bias):
    depth = w_in.shape[0]
    b, t, d = x_prompt.shape
    db, dt, _ = x_sample.shape
    d_sc, d_cf = sc_w.shape[2], cf_w.shape[2]
    d_att = (w_in.shape[2] - 4 * d_sc - 3 * d_cf) // 4
    nh, hd = cache_k.shape[3], cache_k.shape[4]
    w_out_b = w_out.astype(BF16)
    sc_hist = lambda st: st[:, SC_HALO - (SC_WIDTH - 1):]
    cf_hist = lambda st: st[:, CF_HALO - (CF_WIDTH - 1):]
    tm_out_p, tm_out_s = _proj_tiles(b * t)[1], _proj_tiles(db * dt)[1]

    hp, hs = x_prompt, x_sample
    kv_p = (jnp.zeros((depth, b, t, d_att), F32),) * 2
    kv_s = (jnp.zeros((depth, db, dt, d_att), F32),) * 2
    outs = [[] for _ in range(4)]
    for l in range(depth):
        conv_w = (sc_w[l], cf_w[l], cf_b[l], cf_ln_g[l], cf_ln_b[l], d_att, d_sc, d_cf)

        q, kv_p, pr = _project_in(hp, l, kv_p, w_in, norm_pre[l], d_att)
        mix_att, kmean = _prompt_attention(q, *kv_p, pr, rel_bias, l, cache_k, page_table)
        mix_conv, scp, cfp = _conv_mixers(pr, jnp.zeros((b, SC_HALO, d_sc), F32),
                                          jnp.zeros((b, CF_HALO, d_cf), F32), *conv_w, tt=256)

        q_s, kv_s, pr_s = _project_in(hs, l, kv_s, w_in, norm_pre[l], d_att)
        topk = _sample_topk(q_s, kmean.transpose(0, 2, 1, 3))
        hp, mix_att_s = _out_proj(
            mix_att.reshape(b * t, d_att), mix_conv.reshape(b * t, -1), w_out_b, l,
            hp.reshape(b * t, d), norm_post[l], tm_out_p, 512,
            sample=(q_s, *kv_s, pr_s, cache_k, cache_v, page_table, topk, rel_bias))
        hp = hp.reshape(b, t, d)

        mix_conv_s, scs, cfs = _conv_mixers(pr_s, _pad_history(state_sc[l], SC_HALO),
                                            _pad_history(state_cf[l], CF_HALO), *conv_w, tt=dt)
        hs = _out_proj(mix_att_s.reshape(db * dt, d_att), mix_conv_s.reshape(db * dt, -1), w_out_b, l,
                       hs.reshape(db * dt, d), norm_post[l], tm_out_s, 512).reshape(db, dt, d)
        for lst, val in zip(outs, (sc_hist(scp), sc_hist(scs), cf_hist(cfp), cf_hist(cfs))):
            lst.append(val)

    scp, scs, cfp, cfs = (jnp.stack(o) for o in outs)
    return (hp, hs,
            kv_p[0].reshape(depth, b, t, nh, hd), kv_p[1].reshape(depth, b, t, nh, hd),
            kv_s[0].reshape(depth, db, dt, nh, hd), kv_s[1].reshape(depth, db, dt, nh, hd),
            scp, scs, cfp, cfs)
```

```python
import functools
import math

import jax
import jax.numpy as jnp
from jax import lax
from jax.experimental import pallas as pl
from jax.experimental.pallas import tpu as pltpu

F32 = jnp.float32
BF16 = jnp.bfloat16
I32 = jnp.int32

HEAD_DIM = 128
PAGE_SIZE = 128
SC_WIDTH = 3
CF_WIDTH = 31
MOBA_BLOCK = 256
MOBA_TOPK = 3
N_BUCKETS = 32
MAX_DISTANCE = 128
RMS_EPS = 1e-6
LN_EPS = 1e-5
NEG_INF = -1e30

PAGES_PER_BLOCK = MOBA_BLOCK // PAGE_SIZE
SUBLANE = 8
BF16_SUBLANE = 16
LANE = 128
SC_HALO = -(-(SC_WIDTH - 1) // SUBLANE) * SUBLANE
CF_HALO = -(-(CF_WIDTH - 1) // SUBLANE) * SUBLANE
CONV_ROWS = 128

VMEM_LIMIT = 60 * 1024 * 1024


def _params(*sem):
    return pltpu.CompilerParams(dimension_semantics=sem, vmem_limit_bytes=VMEM_LIMIT)


def _silu(x):
    return x * jax.nn.sigmoid(x)


def _rmsnorm_kernel(x_ref, g_ref, o_ref):
    x = x_ref[...]
    ms = jnp.mean(x * x, axis=-1, keepdims=True)
    o_ref[...] = (x * lax.rsqrt(ms + RMS_EPS) * g_ref[...]).astype(o_ref.dtype)


def _rmsnorm(x, g, tm):
    m, d = x.shape
    return pl.pallas_call(
        _rmsnorm_kernel,
        grid=(m // tm,),
        in_specs=[pl.BlockSpec((tm, d), lambda i: (i, 0)),
                  pl.BlockSpec((1, d), lambda i: (0, 0))],
        out_specs=pl.BlockSpec((tm, d), lambda i: (i, 0)),
        out_shape=jax.ShapeDtypeStruct((m, d), BF16),
        compiler_params=_params("parallel"),
        name="rmsnorm_pre",
    )(x, g.reshape(1, d))


def _matmul_kernel(a_ref, w_ref, *rest):
    o_ref = rest[-1]
    o_ref[...] = jnp.dot(a_ref[...], w_ref[...].astype(BF16), preferred_element_type=F32)


def _matmul_cols(a, w, layer, col0, ncols, tm, tn, stack=None):
    m, k = a.shape
    c0 = col0 // tn
    in_specs = [pl.BlockSpec((tm, k), lambda i, j: (i, 0)),
                pl.BlockSpec((None, k, tn), lambda i, j: (layer, 0, c0 + j))]
    args = [a, w]
    aliases = {}
    if stack is None:
        out_spec = pl.BlockSpec((tm, tn), lambda i, j: (i, j))
        out_shape = jax.ShapeDtypeStruct((m, ncols), F32)
    else:
        out_spec = pl.BlockSpec((None, tm, tn), lambda i, j: (layer, i, j))
        out_shape = jax.ShapeDtypeStruct(stack.shape, F32)
        in_specs.append(pl.BlockSpec(memory_space=pl.ANY))
        args.append(stack)
        aliases = {2: 0}
    return pl.pallas_call(
        _matmul_kernel,
        grid=(m // tm, ncols // tn),
        in_specs=in_specs,
        out_specs=out_spec,
        out_shape=out_shape,
        input_output_aliases=aliases,
        compiler_params=_params("parallel", "parallel"),
        name="in_proj",
    )(*args)


def _t5_bias(dist, rb_ref, h):
    max_exact = N_BUCKETS // 2
    d = jnp.maximum(dist, 0)
    ratio = jnp.log(jnp.maximum(d, 1).astype(F32) / max_exact) / math.log(MAX_DISTANCE / max_exact)
    large = max_exact + (ratio * (N_BUCKETS - max_exact)).astype(I32)
    large = jnp.minimum(large, N_BUCKETS - 1)
    bucket = jnp.where(d < max_exact, d, large)
    bias = jnp.zeros(dist.shape, F32)
    for b in range(N_BUCKETS):
        bias = jnp.where(bucket == b, rb_ref[h, b], bias)
    return bias


def _split_bf16(x):
    hi = x.astype(BF16)
    lo = (x - hi.astype(F32)).astype(BF16)
    return hi, lo


_NT = (((1,), (1,)), ((), ()))


def _block_means(page_refs, o_ref):
    for n in range(o_ref.shape[0]):
        total = jnp.zeros(o_ref.shape[1:], F32)
        for p in range(PAGES_PER_BLOCK):
            total = total + jnp.sum(page_refs[n * PAGES_PER_BLOCK + p][...], axis=0)
        o_ref[n] = total * (1.0 / MOBA_BLOCK)


def _prompt_attn_kernel(pt_ref, rb_ref, q_ref, k_ref, v_ref, z_ref, *refs):
    qb_ref, kb_ref, vt_ref, sel_ref, bias_ref, s_ref, p_ref = refs[-7:]
    o_ref, km_ref = refs[-9:-7]
    _block_means(refs[:-9], km_ref)

    h, b = pl.program_id(0), pl.program_id(1)
    t = q_ref.shape[0]
    nb = t // MOBA_BLOCK
    blk = MOBA_BLOCK
    key = lax.broadcasted_iota(I32, (blk, blk), 0)
    qry = lax.broadcasted_iota(I32, (blk, blk), 1)

    @pl.when(b == 0)
    def _():
        bias_ref[0] = _t5_bias(qry - key, rb_ref, h)
        bias_ref[1] = _t5_bias(qry - key + blk, rb_ref, h)
    far_bias = rb_ref[h, N_BUCKETS - 1]

    kf = k_ref[...]
    qs = q_ref[...] * (HEAD_DIM ** -0.5)
    qh, ql = _split_bf16(qs)
    qb_ref[...] = qh
    kb_ref[...] = kf.astype(BF16)
    for j in range(nb):
        cols = slice(j * blk, (j + 1) * blk)
        vt_ref[:, cols] = v_ref[cols, :].T.astype(BF16)

    kmean = jnp.mean(kf.reshape(nb, blk, HEAD_DIM), axis=1)
    nbp = sel_ref.shape[0]
    kmean = jnp.concatenate([kmean, jnp.zeros((nbp - nb, HEAD_DIM), F32)], axis=0)
    kh, kl = _split_bf16(kmean)
    score = (lax.dot_general(kh, qh, _NT, preferred_element_type=F32)
             + lax.dot_general(kl, qh, _NT, preferred_element_type=F32)
             + lax.dot_general(kh, ql, _NT, preferred_element_type=F32))
    blk_id = lax.broadcasted_iota(I32, (nbp, t), 0)
    own = lax.broadcasted_iota(I32, (nbp, t), 1) // blk
    past = blk_id < own
    blk_f = blk_id.astype(F32)
    sc = jnp.where(past, score, NEG_INF)
    picked = jnp.zeros((nbp, t), F32)
    for _ in range(min(MOBA_TOPK, nb)):
        best = jnp.max(sc, axis=0, keepdims=True)
        first = jnp.min(jnp.where(sc == best, blk_f, float(nbp)), axis=0, keepdims=True)
        hit = blk_f == first
        picked = jnp.where(hit, 1.0, picked)
        sc = jnp.where(hit, -jnp.inf, sc)
    sel_ref[...] = jnp.where(past, picked, 0.0)

    for i in range(nb):
        rows = slice(i * blk, (i + 1) * blk)
        qi = qb_ref[rows, :]
        m = None
        for j in range(i + 1):
            cols = slice(j * blk, (j + 1) * blk)
            s = lax.dot_general(kb_ref[cols, :], qi, _NT, preferred_element_type=F32)
            if j == i:
                s = jnp.where(key <= qry, s + bias_ref[0], NEG_INF)
            else:
                s = s + (bias_ref[1] if j == i - 1 else far_bias)
                s = jnp.where(sel_ref[j:j + 1, rows] > 0.0, s, NEG_INF)
            s_ref[cols, :] = s
            mj = jnp.max(s, axis=0, keepdims=True)
            m = mj if m is None else jnp.maximum(m, mj)
        l = jnp.zeros((1, blk), F32)
        for j in range(i + 1):
            cols = slice(j * blk, (j + 1) * blk)
            p = jnp.exp(s_ref[cols, :] - m)
            l = l + jnp.sum(p, axis=0, keepdims=True)
            p_ref[cols, :] = p.astype(BF16)
        n = (i + 1) * blk
        acc = jnp.dot(vt_ref[:, 0:n], p_ref[0:n, :], preferred_element_type=F32)
        o_ref[rows, :] = ((acc / l).T * _silu(z_ref[rows, :])).astype(o_ref.dtype)


def _prompt_attention(q, k_stack, v_stack, pr, rel_bias, layer, cache, page_table):
    b, t, d_att = q.shape
    nh = d_att // HEAD_DIM
    nb = t // MOBA_BLOCK
    nbp = -(-nb // BF16_SUBLANE) * BF16_SUBLANE
    _, _, page, ch, cd = cache.shape
    db, n_pages = page_table.shape
    pages_per_step = db * n_pages // (nh * b)
    assert pages_per_step * nh * b == db * n_pages and n_pages % pages_per_step == 0
    assert pages_per_step % PAGES_PER_BLOCK == 0
    steps_per_seq = n_pages // pages_per_step
    blocks_per_step = pages_per_step // PAGES_PER_BLOCK

    def page_spec(n):
        def imap(hi, bi, pt_ref):
            return (layer, pt_ref[(hi * b + bi) * pages_per_step + n], 0, 0, 0)
        return pl.BlockSpec((None, None, page, ch, cd), imap)

    def km_map(hi, bi, pt_ref):
        step = hi * b + bi
        return (step // steps_per_seq, step % steps_per_seq, 0, 0)

    spec = pl.BlockSpec((None, t, HEAD_DIM), lambda hi, bi, pt_ref: (bi, 0, hi))
    kv_spec = pl.BlockSpec((None, None, t, HEAD_DIM), lambda hi, bi, pt_ref: (layer, bi, 0, hi))
    return pl.pallas_call(
        _prompt_attn_kernel,
        grid_spec=pltpu.PrefetchScalarGridSpec(
            num_scalar_prefetch=1,
            grid=(nh, b),
            in_specs=[pl.BlockSpec(memory_space=pltpu.SMEM), spec, kv_spec, kv_spec, spec]
                     + [page_spec(n) for n in range(pages_per_step)],
            out_specs=[spec, pl.BlockSpec((None, blocks_per_step, ch, cd), km_map)],
            scratch_shapes=[pltpu.VMEM((t, HEAD_DIM), BF16),
                            pltpu.VMEM((t, HEAD_DIM), BF16),
                            pltpu.VMEM((HEAD_DIM, t), BF16),
                            pltpu.VMEM((nbp, t), F32),
                            pltpu.VMEM((2, MOBA_BLOCK, MOBA_BLOCK), F32),
                            pltpu.VMEM((t, MOBA_BLOCK), F32),
                            pltpu.VMEM((t, MOBA_BLOCK), BF16)],
        ),
        out_shape=[jax.ShapeDtypeStruct((b, t, d_att), BF16),
                   jax.ShapeDtypeStruct((db, n_pages // PAGES_PER_BLOCK, ch, cd), F32)],
        compiler_params=_params("arbitrary", "arbitrary"),
        name="prompt_attn",
    )(page_table.reshape(-1), rel_bias, q, k_stack, v_stack, pr, *([cache] * pages_per_step))


def _conv_kernel(scw_ref, cfw_ref, cfb_ref, lng_ref, lnb_ref, psc_ref, pcf_ref,
                 sc_in_ref, sc_b_ref, sc_c_ref, z_sc_ref, cf_a_ref, cf_g_ref, z_cf_ref,
                 mix_ref, sc_state_ref, cf_state_ref, u_ext, g_ext, y_scr):
    tt, d_sc = sc_in_ref.shape
    d_cf = cf_a_ref.shape[1]

    @pl.when(pl.program_id(1) == 0)
    def _():
        g_ext[...] = jnp.zeros_like(g_ext)
        u_ext[0:SC_HALO, :] = psc_ref[...]
        g_ext[0:CF_HALO, :] = pcf_ref[...]

    u_ext[SC_HALO:SC_HALO + tt, :] = sc_c_ref[...] * sc_in_ref[...]
    g_ext[CF_HALO:CF_HALO + tt, :] = cf_a_ref[...] * jax.nn.sigmoid(cf_g_ref[...])

    sc_y = jnp.zeros((tt, d_sc), F32)
    for kk in range(SC_WIDTH):
        off = SC_HALO - (SC_WIDTH - 1) + kk
        sc_y = sc_y + scw_ref[kk:kk + 1, :] * u_ext[off:off + tt, :]
    mix_ref[:, 0:d_sc] = (sc_b_ref[...] * sc_y * _silu(z_sc_ref[...])).astype(mix_ref.dtype)

    first = CF_HALO - (CF_WIDTH - 1)
    sub = min(tt, CONV_ROWS)
    for c in range(d_cf // LANE):
        lanes = slice(c * LANE, (c + 1) * LANE)
        for r0 in range(0, tt, sub):
            acc = jnp.zeros((sub, LANE), F32)
            for rem in range(SUBLANE):
                part = None
                for base in range(0, CF_HALO + SUBLANE, SUBLANE):
                    kk = base + rem - first
                    if 0 <= kk < CF_WIDTH:
                        term = (cfw_ref[kk:kk + 1, lanes]
                                * g_ext[r0 + base:r0 + base + sub + SUBLANE, lanes])
                        part = term if part is None else part + term
                if part is not None:
                    acc = acc + part[rem:rem + sub, :]
            y_scr[r0:r0 + sub, lanes] = acc + cfb_ref[:, lanes]
    y = y_scr[...]
    mu = jnp.mean(y, axis=-1, keepdims=True)
    var = jnp.mean(jnp.square(y - mu), axis=-1, keepdims=True)
    ln = (y - mu) * lax.rsqrt(var + LN_EPS) * lng_ref[...] + lnb_ref[...]
    mix_ref[:, d_sc:d_sc + d_cf] = (_silu(ln) * _silu(z_cf_ref[...])).astype(mix_ref.dtype)

    sc_tail = u_ext[tt:tt + SC_HALO, :]
    cf_tail = g_ext[tt:tt + CF_HALO, :]
    sc_state_ref[...] = sc_tail
    cf_state_ref[...] = cf_tail
    u_ext[0:SC_HALO, :] = sc_tail
    g_ext[0:CF_HALO, :] = cf_tail


def _conv_mixers(pr, prev_sc, prev_cf, sc_w, cf_w, cf_b, ln_g, ln_b, d_att, d_sc, d_cf, tt):
    b, t, _ = pr.shape
    assert d_sc == d_cf and tt % min(tt, CONV_ROWS) == 0
    c0 = d_att // d_sc
    col = lambda n: pl.BlockSpec((None, tt, d_sc), lambda bi, ti, n=n: (bi, ti, c0 + n))
    full = lambda a: pl.BlockSpec(a.shape, lambda bi, ti: (0,) * a.ndim)
    state = lambda rows, d: pl.BlockSpec((None, rows, d), lambda bi, ti: (bi, 0, 0))
    cf_b, ln_g, ln_b = (a.reshape(1, d_cf) for a in (cf_b, ln_g, ln_b))
    ext_rows = lambda halo: -(-(halo + tt) // SUBLANE) * SUBLANE + SUBLANE
    return pl.pallas_call(
        _conv_kernel,
        grid=(b, t // tt),
        in_specs=[full(sc_w), full(cf_w), full(cf_b), full(ln_g), full(ln_b),
                  state(SC_HALO, d_sc), state(CF_HALO, d_cf)] + [col(n) for n in range(7)],
        out_specs=[pl.BlockSpec((None, tt, d_sc + d_cf), lambda bi, ti: (bi, ti, 0)),
                   state(SC_HALO, d_sc), state(CF_HALO, d_cf)],
        out_shape=[jax.ShapeDtypeStruct((b, t, d_sc + d_cf), BF16),
                   jax.ShapeDtypeStruct((b, SC_HALO, d_sc), F32),
                   jax.ShapeDtypeStruct((b, CF_HALO, d_cf), F32)],
        scratch_shapes=[pltpu.VMEM((ext_rows(SC_HALO), d_sc), F32),
                        pltpu.VMEM((ext_rows(CF_HALO), d_cf), F32),
                        pltpu.VMEM((tt, d_cf), F32)],
        compiler_params=_params("parallel", "arbitrary"),
        name="conv_mixers",
    )(sc_w, cf_w, cf_b, ln_g, ln_b, prev_sc, prev_cf, *([pr] * 7))


def _pad_history(state, halo):
    return jnp.pad(state, ((0, 0), (halo - state.shape[1], 0), (0, 0)))


def _sample_topk_kernel(q_ref, km_ref, o_ref):
    nq = q_ref.shape[0]
    nh, nb, _ = km_ref.shape
    blk_f = lax.broadcasted_iota(I32, (nb, 1), 0).astype(F32)
    row = lax.broadcasted_iota(I32, (nq, LANE), 0)
    lane = lax.broadcasted_iota(I32, (nq, LANE), 1)
    for h in range(nh):
        cols = slice(h * HEAD_DIM, (h + 1) * HEAD_DIM)
        km = km_ref[h]
        out = jnp.zeros((nq, LANE), F32)
        for qi in range(nq):
            qv = q_ref[qi:qi + 1, cols] * (HEAD_DIM ** -0.5)
            sc = jnp.sum(km * qv, axis=1, keepdims=True)
            for r in range(MOBA_TOPK):
                best = jnp.max(sc, axis=0, keepdims=True)
                first = jnp.min(jnp.where(sc == best, blk_f, float(nb)), axis=0, keepdims=True)
                out = jnp.where((row == qi) & (lane == r), first, out)
                sc = jnp.where(blk_f == first, -jnp.inf, sc)
        o_ref[h] = out.astype(I32)


def _sample_topk(q, kmean):
    db, t, d_att = q.shape
    _, nh, nb, hd = kmean.shape
    out = pl.pallas_call(
        _sample_topk_kernel,
        grid=(db,),
        in_specs=[pl.BlockSpec((None, t, d_att), lambda bi: (bi, 0, 0)),
                  pl.BlockSpec((None, nh, nb, hd), lambda bi: (bi, 0, 0, 0))],
        out_specs=pl.BlockSpec((None, nh, t, LANE), lambda bi: (bi, 0, 0, 0)),
        out_shape=jax.ShapeDtypeStruct((db, nh, t, LANE), I32),
        compiler_params=_params("parallel"),
        name="sample_topk",
    )(q, kmean)
    return out[..., :MOBA_TOPK]


def _last_page_bias_kernel(rb_ref, o_ref):
    nh, nq, page = o_ref.shape
    qi = lax.broadcasted_iota(I32, (nq, page), 0)
    row = lax.broadcasted_iota(I32, (nq, page), 1)
    for h in range(nh):
        o_ref[h] = _t5_bias(page + qi - row, rb_ref, h)


def _last_page_bias(rel_bias, nq):
    nh = rel_bias.shape[0]
    out = pl.pallas_call(
        _last_page_bias_kernel,
        in_specs=[pl.BlockSpec(memory_space=pltpu.SMEM)],
        out_shape=jax.ShapeDtypeStruct((nh, nq, PAGE_SIZE), F32),
        name="last_page_bias",
    )(rel_bias)
    return out[..., None]


class _SampleGather:
    def __init__(self, topk_ref, pt_ref, ck_hbm, cv_hbm, kbuf, vbuf, sems, *, layer, n_pages, nh, nq):
        self.topk_ref, self.pt_ref = topk_ref, pt_ref
        self.src = (ck_hbm, cv_hbm)
        self.dst = (kbuf, vbuf)
        self.sems = sems
        self.layer, self.n_pages, self.nh, self.nq = layer, n_pages, nh, nq
        self.n_sel = MOBA_TOPK * PAGES_PER_BLOCK
        self.slots = [(qi, s, p) for qi in range(nq) for s in range(MOBA_TOPK)
                      for p in range(PAGES_PER_BLOCK)]

    def chosen_block(self, g, qi, s):
        return self.topk_ref[(g * self.nq + qi) * MOBA_TOPK + s]

    def slot_index(self, qi, s, p):
        return qi * self.n_sel + s * PAGES_PER_BLOCK + p

    def _copies(self, g, half, qi, s, p):
        page = self.pt_ref[(g // self.nh) * self.n_pages
                           + self.chosen_block(g, qi, s) * PAGES_PER_BLOCK + p]
        n = self.slot_index(qi, s, p)
        head = g % self.nh
        return [pltpu.make_async_copy(src.at[self.layer, page, :, head, :], dst.at[half, n],
                                      self.sems.at[which, half, n])
                for which, (src, dst) in enumerate(zip(self.src, self.dst))]

    def start(self, g, half):
        for slot in self.slots:
            for cp in self._copies(g, half, *slot):
                cp.start()

    def wait(self, g, half):
        for slot in self.slots:
            for cp in self._copies(g, half, *slot):
                cp.wait()


def _sample_attend(gather, step, rb_ref, near_ref, q_ref, kn_ref, vn_ref, z_ref, o_ref, *, n_blocks):
    nq = gather.nq
    h = step % gather.nh
    half = step % 2
    kbuf, vbuf = gather.dst
    nrow = lax.broadcasted_iota(I32, (nq, 1), 0)
    far_bias = rb_ref[h, N_BUCKETS - 1]
    for qi in range(nq):
        qv = q_ref[qi:qi + 1, :] * (HEAD_DIM ** -0.5)
        logits = []
        for s in range(MOBA_TOPK):
            blk = gather.chosen_block(step, qi, s)
            for p in range(PAGES_PER_BLOCK):
                kp = kbuf[half, gather.slot_index(qi, s, p)]
                lg = jnp.sum(kp * qv, axis=1, keepdims=True)
                if p == PAGES_PER_BLOCK - 1:
                    bias = jnp.where(blk == n_blocks - 1, near_ref[qi], far_bias)
                else:
                    bias = far_bias
                logits.append((lg + bias, gather.slot_index(qi, s, p)))
        own = jnp.sum(kn_ref[...] * qv, axis=1, keepdims=True) + _t5_bias(qi - nrow, rb_ref, h)
        own = jnp.where(nrow <= qi, own, NEG_INF)
        m = jnp.max(own, axis=0, keepdims=True)
        for lg, _ in logits:
            m = jnp.maximum(m, jnp.max(lg, axis=0, keepdims=True))
        p_own = jnp.exp(own - m)
        l = jnp.sum(p_own, axis=0, keepdims=True)
        acc = jnp.sum(p_own * vn_ref[...], axis=0, keepdims=True)
        for lg, n in logits:
            pn = jnp.exp(lg - m)
            l = l + jnp.sum(pn, axis=0, keepdims=True)
            acc = acc + jnp.sum(pn * vbuf[half, n], axis=0, keepdims=True)
        o_ref[qi:qi + 1, :] = acc / l * _silu(z_ref[qi:qi + 1, :])


def _out_proj_step(ma_ref, mc_ref, wa_ref, wc_ref, g_ref, x_hbm, o_ref, ss_ref, x_buf, x_sem):
    i, j = pl.program_id(0), pl.program_id(1)
    tm = o_ref.shape[0]
    tn = wa_ref.shape[1]

    def x_copy():
        return pltpu.make_async_copy(x_hbm.at[pl.ds(pl.multiple_of(i * tm, tm), tm), :], x_buf, x_sem)

    @pl.when(j == 0)
    def _():
        ss_ref[...] = jnp.zeros_like(ss_ref)
        x_copy().start()

    y = (jnp.dot(ma_ref[...].astype(BF16), wa_ref[...], preferred_element_type=F32)
         + jnp.dot(mc_ref[...].astype(BF16), wc_ref[...], preferred_element_type=F32))
    ss_ref[...] += jnp.sum(y * y, axis=-1, keepdims=True)
    o_ref[:, pl.ds(pl.multiple_of(j * tn, tn), tn)] = y

    @pl.when(j == pl.num_programs(1) - 1)
    def _():
        x_copy().wait()
        inv = lax.rsqrt(ss_ref[...] / o_ref.shape[1] + RMS_EPS)
        o_ref[...] = x_buf[...] + o_ref[...] * inv * g_ref[...]


def _out_proj_kernel(*refs):
    _out_proj_step(*refs)


def _out_proj_hosting_kernel(topk_ref, pt_ref, ma_ref, mc_ref, wa_ref, wc_ref, g_ref, x_hbm,
                             rb_ref, near_ref, q_ref, kn_ref, vn_ref, z_ref, ck_hbm, cv_hbm,
                             o_ref, mix_ref, ss_ref, x_buf, x_sem, kbuf, vbuf, sems,
                             *, layer, n_pages, nh):
    step = pl.program_id(0) * pl.num_programs(1) + pl.program_id(1)
    n_steps = pl.num_programs(0) * pl.num_programs(1)
    half = step % 2
    gather = _SampleGather(topk_ref, pt_ref, ck_hbm, cv_hbm, kbuf, vbuf, sems,
                           layer=layer, n_pages=n_pages, nh=nh, nq=q_ref.shape[0])

    @pl.when(step == 0)
    def _():
        gather.start(step, half)

    @pl.when(step + 1 < n_steps)
    def _():
        gather.start(step + 1, 1 - half)

    _out_proj_step(ma_ref, mc_ref, wa_ref, wc_ref, g_ref, x_hbm, o_ref, ss_ref, x_buf, x_sem)

    gather.wait(step, half)
    _sample_attend(gather, step, rb_ref, near_ref, q_ref, kn_ref, vn_ref, z_ref, mix_ref,
                   n_blocks=n_pages // PAGES_PER_BLOCK)


def _out_proj(mix_att, mix_conv, w_out, layer, x, g, tm, tn, sample=None):
    m, d = x.shape
    ka, kc = mix_att.shape[1], mix_conv.shape[1]
    assert ka == kc
    grid = (m // tm, d // tn)
    in_specs = [pl.BlockSpec((tm, ka), lambda i, j, *_: (i, 0)),
                pl.BlockSpec((tm, kc), lambda i, j, *_: (i, 0)),
                pl.BlockSpec((None, ka, tn), lambda i, j, *_: (layer, 0, j)),
                pl.BlockSpec((None, kc, tn), lambda i, j, *_: (layer, 1, j)),
                pl.BlockSpec((1, d), lambda i, j, *_: (0, 0)),
                pl.BlockSpec(memory_space=pl.ANY)]
    out_spec = pl.BlockSpec((tm, d), lambda i, j, *_: (i, 0))
    out_shape = jax.ShapeDtypeStruct((m, d), F32)
    scratch = [pltpu.VMEM((tm, 1), F32), pltpu.VMEM((tm, d), F32), pltpu.SemaphoreType.DMA(())]
    args = (mix_att, mix_conv, w_out, w_out, g.reshape(1, d), x)
    if sample is None:
        return pl.pallas_call(
            _out_proj_kernel, grid=grid, in_specs=in_specs, out_specs=out_spec, out_shape=out_shape,
            scratch_shapes=scratch, compiler_params=_params("arbitrary", "arbitrary"),
            name="out_proj",
        )(*args)

    q, k_stack, v_stack, pr, cache_k, cache_v, page_table, topk, rel_bias, near_bias = sample
    db, t, d_att = q.shape
    nh = d_att // HEAD_DIM
    n_pages = page_table.shape[1]
    assert (n_pages * PAGE_SIZE) % MOBA_BLOCK == 0 and t <= MOBA_BLOCK
    assert grid[0] * grid[1] == db * nh, "one sample (batch, head) per grid step"
    assert PAGE_SIZE + 1 >= MAX_DISTANCE
    n_buf = t * MOBA_TOPK * PAGES_PER_BLOCK

    def head_map(i, j, *_):
        step = i * grid[1] + j
        return (step // nh, 0, step % nh)

    head = pl.BlockSpec((None, t, HEAD_DIM), head_map)
    new_rows = pl.BlockSpec((None, None, t, HEAD_DIM), lambda i, j, *_: (layer,) + head_map(i, j))
    near = pl.BlockSpec((None, t, PAGE_SIZE, 1), lambda i, j, *_: (head_map(i, j)[2], 0, 0, 0))
    hbm = pl.BlockSpec(memory_space=pl.ANY)
    return pl.pallas_call(
        functools.partial(_out_proj_hosting_kernel, layer=layer, n_pages=n_pages, nh=nh),
        grid_spec=pltpu.PrefetchScalarGridSpec(
            num_scalar_prefetch=2,
            grid=grid,
            in_specs=in_specs + [pl.BlockSpec(memory_space=pltpu.SMEM), near, head, new_rows,
                                 new_rows, head, hbm, hbm],
            out_specs=[out_spec, head],
            scratch_shapes=scratch + [pltpu.VMEM((2, n_buf, PAGE_SIZE, HEAD_DIM), F32),
                                      pltpu.VMEM((2, n_buf, PAGE_SIZE, HEAD_DIM), F32),
                                      pltpu.SemaphoreType.DMA((2, 2, n_buf))],
        ),
        out_shape=[out_shape, jax.ShapeDtypeStruct((db, t, d_att), F32)],
        compiler_params=_params("arbitrary", "arbitrary"),
        name="out_proj_sample_attn",
    )(topk.reshape(-1), page_table.reshape(-1), *args, rel_bias, near_bias, q, k_stack, v_stack, pr,
      cache_k, cache_v)


def _proj_tiles(m):
    if m % 2048 == 0:
        return (2048, 256), 512
    return (m, 1024), m


def _project_in(x, layer, kv_stack, w_in, g_pre, d_att):
    b, t, d = x.shape
    m = b * t
    (tm, tn), tm_norm = _proj_tiles(m)
    h = _rmsnorm(x.reshape(m, d), g_pre, min(tm_norm, 256))
    q = _matmul_cols(h, w_in, layer, 0, d_att, tm, tn).reshape(b, t, d_att)
    k_stack, v_stack = (
        _matmul_cols(h, w_in, layer, (n + 1) * d_att, d_att, tm, tn,
                     stack=kv_stack[n].reshape(-1, m, d_att)).reshape(-1, b, t, d_att)
        for n in range(2))
    pr = _matmul_cols(h, w_in, layer, 3 * d_att, w_in.shape[2] - 3 * d_att, tm, tn).reshape(b, t, -1)
    return q, (k_stack, v_stack), pr


def kernel(x_prompt, x_sample, cache_k, cache_v, state_sc, state_cf, page_table, w_in, w_out,
           norm_pre, norm_post, sc_w, cf_w, cf_b, cf_ln_g, cf_ln_b, rel_bias):
    depth = w_in.shape[0]
    b, t, d = x_prompt.shape
    db, dt, _ = x_sample.shape
    d_sc, d_cf = sc_w.shape[2], cf_w.shape[2]
    d_att = (w_in.shape[2] - 4 * d_sc - 3 * d_cf) // 4
    nh, hd = cache_k.shape[3], cache_k.shape[4]
    w_out_b = w_out.astype(BF16)
    near_bias = _last_page_bias(rel_bias, dt)
    sc_hist = lambda st: st[:, SC_HALO - (SC_WIDTH - 1):]
    cf_hist = lambda st: st[:, CF_HALO - (CF_WIDTH - 1):]
    tm_out_p, tm_out_s = _proj_tiles(b * t)[1], _proj_tiles(db * dt)[1]

    hp, hs = x_prompt, x_sample
    kv_p = (jnp.zeros((depth, b, t, d_att), F32),) * 2
    kv_s = (jnp.zeros((depth, db, dt, d_att), F32),) * 2
    outs = [[] for _ in range(4)]
    for l in range(depth):
        conv_w = (sc_w[l], cf_w[l], cf_b[l], cf_ln_g[l], cf_ln_b[l], d_att, d_sc, d_cf)

        q, kv_p, pr = _project_in(hp, l, kv_p, w_in, norm_pre[l], d_att)
        mix_att, kmean = _prompt_attention(q, *kv_p, pr, rel_bias, l, cache_k, page_table)
        mix_conv, scp, cfp = _conv_mixers(pr, jnp.zeros((b, SC_HALO, d_sc), F32),
                                          jnp.zeros((b, CF_HALO, d_cf), F32), *conv_w, tt=256)

        q_s, kv_s, pr_s = _project_in(hs, l, kv_s, w_in, norm_pre[l], d_att)
        topk = _sample_topk(q_s, kmean.transpose(0, 2, 1, 3))
        hp, mix_att_s = _out_proj(
            mix_att.reshape(b * t, d_att), mix_conv.reshape(b * t, -1), w_out_b, l,
            hp.reshape(b * t, d), norm_post[l], tm_out_p, 512,
            sample=(q_s, *kv_s, pr_s, cache_k, cache_v, page_table, topk, rel_bias, near_bias))
        hp = hp.reshape(b, t, d)

        mix_conv_s, scs, cfs = _conv_mixers(pr_s, _pad_history(state_sc[l], SC_HALO),
                                            _pad_history(state_cf[l], CF_HALO), *conv_w, tt=dt)
        hs = _out_proj(mix_att_s.reshape(db * dt, d_att), mix_conv_s.reshape(db * dt, -1), w_out_b, l,
                       hs.reshape(db * dt, d), norm_post[l], tm_out_s, 512).reshape(db, dt, d)
        for lst, val in zip(outs, (sc_hist(scp), sc_hist(scs), cf_hist(cfp), cf_hist(cfs))):
            lst.append(val)

    scp, scs, cfp, cfs = (jnp.stack(o) for o in outs)
    return (hp, hs,
            kv_p[0].reshape(depth, b, t, nh, hd), kv_p[1].reshape(depth, b, t, nh, hd),
            kv_s[0].reshape(depth, db, dt, nh, hd), kv_s[1].reshape(depth, db, dt, nh, hd),
            scp, scs, cfp, cfs)
```

```python
import functools
import math

import jax
import jax.numpy as jnp
from jax import lax
from jax.experimental import pallas as pl
from jax.experimental.pallas import tpu as pltpu

F32 = jnp.float32
BF16 = jnp.bfloat16
I32 = jnp.int32

HEAD_DIM = 128
PAGE_SIZE = 128
SC_WIDTH = 3
CF_WIDTH = 31
MOBA_BLOCK = 256
MOBA_TOPK = 3
N_BUCKETS = 32
MAX_DISTANCE = 128
RMS_EPS = 1e-6
LN_EPS = 1e-5
NEG_INF = -1e30

PAGES_PER_BLOCK = MOBA_BLOCK // PAGE_SIZE
SUBLANE = 8
BF16_SUBLANE = 16
LANE = 128
SC_HALO = -(-(SC_WIDTH - 1) // SUBLANE) * SUBLANE
CF_HALO = -(-(CF_WIDTH - 1) // SUBLANE) * SUBLANE
CONV_ROWS = 128

VMEM_LIMIT = 60 * 1024 * 1024


def _params(*sem):
    return pltpu.CompilerParams(dimension_semantics=sem, vmem_limit_bytes=VMEM_LIMIT)


def _silu(x):
    return x * jax.nn.sigmoid(x)


def _rmsnorm_kernel(x_ref, g_ref, o_ref):
    x = x_ref[...]
    ms = jnp.mean(x * x, axis=-1, keepdims=True)
    o_ref[...] = (x * lax.rsqrt(ms + RMS_EPS) * g_ref[...]).astype(o_ref.dtype)


def _rmsnorm(x, g, tm):
    m, d = x.shape
    return pl.pallas_call(
        _rmsnorm_kernel,
        grid=(m // tm,),
        in_specs=[pl.BlockSpec((tm, d), lambda i: (i, 0)),
                  pl.BlockSpec((1, d), lambda i: (0, 0))],
        out_specs=pl.BlockSpec((tm, d), lambda i: (i, 0)),
        out_shape=jax.ShapeDtypeStruct((m, d), BF16),
        compiler_params=_params("parallel"),
        name="rmsnorm_pre",
    )(x, g.reshape(1, d))


def _block_means(page_refs, o_ref):
    for n in range(o_ref.shape[0]):
        total = jnp.zeros(o_ref.shape[1:], F32)
        for p in range(PAGES_PER_BLOCK):
            total = total + jnp.sum(page_refs[n * PAGES_PER_BLOCK + p][...], axis=0)
        o_ref[n] = total * (1.0 / MOBA_BLOCK)


class _BlockMeanShare:
    def __init__(self, cache, layer, first_page, n_pages, n_steps, step_of):
        _, _, page, nh, hd = cache.shape
        per_step = -(-n_pages // n_steps)
        self.pages_per_step = -(-per_step // PAGES_PER_BLOCK) * PAGES_PER_BLOCK
        assert n_pages % self.pages_per_step == 0
        self.n_host_steps = n_pages // self.pages_per_step
        self.all_steps = self.n_host_steps == n_steps
        blocks_per_step = self.pages_per_step // PAGES_PER_BLOCK
        host_step = lambda *idx: jnp.minimum(step_of(*idx), self.n_host_steps - 1)

        def page_spec(n):
            def imap(*idx):
                *grid_idx, pt_ref = idx
                return (layer, pt_ref[first_page + host_step(*grid_idx) * self.pages_per_step + n],
                        0, 0, 0)
            return pl.BlockSpec((None, None, page, nh, hd), imap)

        self.in_specs = [page_spec(n) for n in range(self.pages_per_step)]
        self.args = [cache] * self.pages_per_step
        self.out_spec = pl.BlockSpec((blocks_per_step, nh, hd),
                                     lambda *idx: (host_step(*idx[:-1]), 0, 0))
        self.out_shape = jax.ShapeDtypeStruct((n_pages // PAGES_PER_BLOCK, nh, hd), F32)

    def reduce(self, step, page_refs, o_ref):
        if self.all_steps:
            _block_means(page_refs, o_ref)
        else:
            pl.when(step < self.n_host_steps)(lambda: _block_means(page_refs, o_ref))


def _matmul_kernel(*refs, share=None):
    if share is None:
        a_ref, w_ref, o_ref = refs[0], refs[1], refs[-1]
    else:
        _, a_ref, w_ref = refs[:3]
        o_ref, km_ref = refs[-2:]
        step = pl.program_id(0) * pl.num_programs(1) + pl.program_id(1)
        share.reduce(step, refs[3:-2], km_ref)
    o_ref[...] = jnp.dot(a_ref[...], w_ref[...].astype(BF16), preferred_element_type=F32)


def _matmul_cols(a, w, layer, col0, ncols, tm, tn, stack=None, means=None):
    m, k = a.shape
    c0 = col0 // tn
    grid = (m // tm, ncols // tn)
    in_specs = [pl.BlockSpec((tm, k), lambda i, j, *_: (i, 0)),
                pl.BlockSpec((None, k, tn), lambda i, j, *_: (layer, 0, c0 + j))]
    args = [a, w]
    aliases = {}
    if stack is None:
        out_spec = pl.BlockSpec((tm, tn), lambda i, j, *_: (i, j))
        out_shape = jax.ShapeDtypeStruct((m, ncols), F32)
    else:
        out_spec = pl.BlockSpec((None, tm, tn), lambda i, j: (layer, i, j))
        out_shape = jax.ShapeDtypeStruct(stack.shape, F32)
        in_specs.append(pl.BlockSpec(memory_space=pl.ANY))
        args.append(stack)
        aliases = {2: 0}
    if means is None:
        return pl.pallas_call(
            _matmul_kernel,
            grid=grid,
            in_specs=in_specs,
            out_specs=out_spec,
            out_shape=out_shape,
            input_output_aliases=aliases,
            compiler_params=_params("parallel", "parallel"),
            name="in_proj",
        )(*args)

    assert stack is None
    cache, page_table, first_page, n_pages = means
    share = _BlockMeanShare(cache, layer, first_page, n_pages, grid[0] * grid[1],
                            lambda i, j: i * grid[1] + j)
    in_specs[0] = pl.BlockSpec((tm, k), lambda i, j, *_: (i, 0), pipeline_mode=pl.Buffered(1))
    return pl.pallas_call(
        functools.partial(_matmul_kernel, share=share),
        grid_spec=pltpu.PrefetchScalarGridSpec(
            num_scalar_prefetch=1,
            grid=grid,
            in_specs=in_specs + share.in_specs,
            out_specs=[out_spec, share.out_spec],
        ),
        out_shape=[out_shape, share.out_shape],
        compiler_params=_params("arbitrary", "arbitrary"),
        name="in_proj_block_means",
    )(page_table.reshape(-1), *args, *share.args)


def _t5_bias(dist, rb_ref, h):
    max_exact = N_BUCKETS // 2
    d = jnp.maximum(dist, 0)
    ratio = jnp.log(jnp.maximum(d, 1).astype(F32) / max_exact) / math.log(MAX_DISTANCE / max_exact)
    large = max_exact + (ratio * (N_BUCKETS - max_exact)).astype(I32)
    large = jnp.minimum(large, N_BUCKETS - 1)
    bucket = jnp.where(d < max_exact, d, large)
    bias = jnp.zeros(dist.shape, F32)
    for b in range(N_BUCKETS):
        bias = jnp.where(bucket == b, rb_ref[h, b], bias)
    return bias


def _split_bf16(x):
    hi = x.astype(BF16)
    lo = (x - hi.astype(F32)).astype(BF16)
    return hi, lo


_NT = (((1,), (1,)), ((), ()))


def _prompt_attn_kernel(pt_ref, rb_ref, q_ref, k_ref, v_ref, z_ref, *refs, share):
    qb_ref, kb_ref, vt_ref, sel_ref, bias_ref, s_ref, p_ref = refs[-7:]
    o_ref, km_ref = refs[-9:-7]
    h, b = pl.program_id(0), pl.program_id(1)
    share.reduce(h * pl.num_programs(1) + b, refs[:-9], km_ref)

    t = q_ref.shape[0]
    nb = t // MOBA_BLOCK
    blk = MOBA_BLOCK
    key = lax.broadcasted_iota(I32, (blk, blk), 0)
    qry = lax.broadcasted_iota(I32, (blk, blk), 1)

    @pl.when(b == 0)
    def _():
        bias_ref[0] = _t5_bias(qry - key, rb_ref, h)
        bias_ref[1] = _t5_bias(qry - key + blk, rb_ref, h)
    far_bias = rb_ref[h, N_BUCKETS - 1]

    kf = k_ref[...]
    qs = q_ref[...] * (HEAD_DIM ** -0.5)
    qh, ql = _split_bf16(qs)
    qb_ref[...] = qh
    kb_ref[...] = kf.astype(BF16)
    for j in range(nb):
        cols = slice(j * blk, (j + 1) * blk)
        vt_ref[:, cols] = v_ref[cols, :].T.astype(BF16)

    kmean = jnp.mean(kf.reshape(nb, blk, HEAD_DIM), axis=1)
    nbp = sel_ref.shape[0]
    kmean = jnp.concatenate([kmean, jnp.zeros((nbp - nb, HEAD_DIM), F32)], axis=0)
    kh, kl = _split_bf16(kmean)
    score = (lax.dot_general(kh, qh, _NT, preferred_element_type=F32)
             + lax.dot_general(kl, qh, _NT, preferred_element_type=F32)
             + lax.dot_general(kh, ql, _NT, preferred_element_type=F32))
    blk_id = lax.broadcasted_iota(I32, (nbp, t), 0)
    own = lax.broadcasted_iota(I32, (nbp, t), 1) // blk
    past = blk_id < own
    blk_f = blk_id.astype(F32)
    sc = jnp.where(past, score, NEG_INF)
    picked = jnp.zeros((nbp, t), F32)
    for _ in range(min(MOBA_TOPK, nb)):
        best = jnp.max(sc, axis=0, keepdims=True)
        first = jnp.min(jnp.where(sc == best, blk_f, float(nbp)), axis=0, keepdims=True)
        hit = blk_f == first
        picked = jnp.where(hit, 1.0, picked)
        sc = jnp.where(hit, -jnp.inf, sc)
    sel_ref[...] = jnp.where(past, picked, 0.0)

    for i in range(nb):
        rows = slice(i * blk, (i + 1) * blk)
        qi = qb_ref[rows, :]
        m = None
        for j in range(i + 1):
            cols = slice(j * blk, (j + 1) * blk)
            s = lax.dot_general(kb_ref[cols, :], qi, _NT, preferred_element_type=F32)
            if j == i:
                s = jnp.where(key <= qry, s + bias_ref[0], NEG_INF)
            else:
                s = s + (bias_ref[1] if j == i - 1 else far_bias)
                s = jnp.where(sel_ref[j:j + 1, rows] > 0.0, s, NEG_INF)
            s_ref[cols, :] = s
            mj = jnp.max(s, axis=0, keepdims=True)
            m = mj if m is None else jnp.maximum(m, mj)
        l = jnp.zeros((1, blk), F32)
        for j in range(i + 1):
            cols = slice(j * blk, (j + 1) * blk)
            p = jnp.exp(s_ref[cols, :] - m)
            l = l + jnp.sum(p, axis=0, keepdims=True)
            p_ref[cols, :] = p.astype(BF16)
        n = (i + 1) * blk
        acc = jnp.dot(vt_ref[:, 0:n], p_ref[0:n, :], preferred_element_type=F32)
        o_ref[rows, :] = ((acc / l).T * _silu(z_ref[rows, :])).astype(o_ref.dtype)


def _prompt_attention(q, k_stack, v_stack, pr, rel_bias, layer, means):
    b, t, d_att = q.shape
    nh = d_att // HEAD_DIM
    nb = t // MOBA_BLOCK
    nbp = -(-nb // BF16_SUBLANE) * BF16_SUBLANE
    cache, page_table, first_page, n_pages = means
    share = _BlockMeanShare(cache, layer, first_page, n_pages, nh * b, lambda hi, bi: hi * b + bi)

    spec = pl.BlockSpec((None, t, HEAD_DIM), lambda hi, bi, pt_ref: (bi, 0, hi))
    kv_spec = pl.BlockSpec((None, None, t, HEAD_DIM), lambda hi, bi, pt_ref: (layer, bi, 0, hi))
    return pl.pallas_call(
        functools.partial(_prompt_attn_kernel, share=share),
        grid_spec=pltpu.PrefetchScalarGridSpec(
            num_scalar_prefetch=1,
            grid=(nh, b),
            in_specs=[pl.BlockSpec(memory_space=pltpu.SMEM), spec, kv_spec, kv_spec, spec]
                     + share.in_specs,
            out_specs=[spec, share.out_spec],
            scratch_shapes=[pltpu.VMEM((t, HEAD_DIM), BF16),
                            pltpu.VMEM((t, HEAD_DIM), BF16),
                            pltpu.VMEM((HEAD_DIM, t), BF16),
                            pltpu.VMEM((nbp, t), F32),
                            pltpu.VMEM((2, MOBA_BLOCK, MOBA_BLOCK), F32),
                            pltpu.VMEM((t, MOBA_BLOCK), F32),
                            pltpu.VMEM((t, MOBA_BLOCK), BF16)],
        ),
        out_shape=[jax.ShapeDtypeStruct((b, t, d_att), BF16), share.out_shape],
        compiler_params=_params("arbitrary", "arbitrary"),
        name="prompt_attn",
    )(page_table.reshape(-1), rel_bias, q, k_stack, v_stack, pr, *share.args)


def _conv_kernel(scw_ref, cfw_ref, cfb_ref, lng_ref, lnb_ref, psc_ref, pcf_ref,
                 sc_in_ref, sc_b_ref, sc_c_ref, z_sc_ref, cf_a_ref, cf_g_ref, z_cf_ref,
                 mix_ref, sc_state_ref, cf_state_ref, u_ext, g_ext, y_scr):
    tt, d_sc = sc_in_ref.shape
    d_cf = cf_a_ref.shape[1]

    @pl.when(pl.program_id(1) == 0)
    def _():
        g_ext[...] = jnp.zeros_like(g_ext)
        u_ext[0:SC_HALO, :] = psc_ref[...]
        g_ext[0:CF_HALO, :] = pcf_ref[...]

    u_ext[SC_HALO:SC_HALO + tt, :] = sc_c_ref[...] * sc_in_ref[...]
    g_ext[CF_HALO:CF_HALO + tt, :] = cf_a_ref[...] * jax.nn.sigmoid(cf_g_ref[...])

    sc_y = jnp.zeros((tt, d_sc), F32)
    for kk in range(SC_WIDTH):
        off = SC_HALO - (SC_WIDTH - 1) + kk
        sc_y = sc_y + scw_ref[kk:kk + 1, :] * u_ext[off:off + tt, :]
    mix_ref[:, 0:d_sc] = (sc_b_ref[...] * sc_y * _silu(z_sc_ref[...])).astype(mix_ref.dtype)

    first = CF_HALO - (CF_WIDTH - 1)
    sub = min(tt, CONV_ROWS)
    for c in range(d_cf // LANE):
        lanes = slice(c * LANE, (c + 1) * LANE)
        for r0 in range(0, tt, sub):
            acc = jnp.zeros((sub, LANE), F32)
            for rem in range(SUBLANE):
                part = None
                for base in range(0, CF_HALO + SUBLANE, SUBLANE):
                    kk = base + rem - first
                    if 0 <= kk < CF_WIDTH:
                        term = (cfw_ref[kk:kk + 1, lanes]
                                * g_ext[r0 + base:r0 + base + sub + SUBLANE, lanes])
                        part = term if part is None else part + term
                if part is not None:
                    acc = acc + part[rem:rem + sub, :]
            y_scr[r0:r0 + sub, lanes] = acc + cfb_ref[:, lanes]
    y = y_scr[...]
    mu = jnp.mean(y, axis=-1, keepdims=True)
    var = jnp.mean(jnp.square(y - mu), axis=-1, keepdims=True)
    ln = (y - mu) * lax.rsqrt(var + LN_EPS) * lng_ref[...] + lnb_ref[...]
    mix_ref[:, d_sc:d_sc + d_cf] = (_silu(ln) * _silu(z_cf_ref[...])).astype(mix_ref.dtype)

    sc_tail = u_ext[tt:tt + SC_HALO, :]
    cf_tail = g_ext[tt:tt + CF_HALO, :]
    sc_state_ref[...] = sc_tail
    cf_state_ref[...] = cf_tail
    u_ext[0:SC_HALO, :] = sc_tail
    g_ext[0:CF_HALO, :] = cf_tail


def _conv_mixers(pr, prev_sc, prev_cf, sc_w, cf_w, cf_b, ln_g, ln_b, d_att, d_sc, d_cf, tt):
    b, t, _ = pr.shape
    assert d_sc == d_cf and tt % min(tt, CONV_ROWS) == 0
    c0 = d_att // d_sc
    col = lambda n: pl.BlockSpec((None, tt, d_sc), lambda bi, ti, n=n: (bi, ti, c0 + n))
    full = lambda a: pl.BlockSpec(a.shape, lambda bi, ti: (0,) * a.ndim)
    state = lambda rows, d: pl.BlockSpec((None, rows, d), lambda bi, ti: (bi, 0, 0))
    cf_b, ln_g, ln_b = (a.reshape(1, d_cf) for a in (cf_b, ln_g, ln_b))
    ext_rows = lambda halo: -(-(halo + tt) // SUBLANE) * SUBLANE + SUBLANE
    return pl.pallas_call(
        _conv_kernel,
        grid=(b, t // tt),
        in_specs=[full(sc_w), full(cf_w), full(cf_b), full(ln_g), full(ln_b),
                  state(SC_HALO, d_sc), state(CF_HALO, d_cf)] + [col(n) for n in range(7)],
        out_specs=[pl.BlockSpec((None, tt, d_sc + d_cf), lambda bi, ti: (bi, ti, 0)),
                   state(SC_HALO, d_sc), state(CF_HALO, d_cf)],
        out_shape=[jax.ShapeDtypeStruct((b, t, d_sc + d_cf), BF16),
                   jax.ShapeDtypeStruct((b, SC_HALO, d_sc), F32),
                   jax.ShapeDtypeStruct((b, CF_HALO, d_cf), F32)],
        scratch_shapes=[pltpu.VMEM((ext_rows(SC_HALO), d_sc), F32),
                        pltpu.VMEM((ext_rows(CF_HALO), d_cf), F32),
                        pltpu.VMEM((tt, d_cf), F32)],
        compiler_params=_params("parallel", "arbitrary"),
        name="conv_mixers",
    )(sc_w, cf_w, cf_b, ln_g, ln_b, prev_sc, prev_cf, *([pr] * 7))


def _pad_history(state, halo):
    return jnp.pad(state, ((0, 0), (halo - state.shape[1], 0), (0, 0)))


def _sample_topk_kernel(q_ref, km_ref, o_ref):
    nq = q_ref.shape[0]
    nh, nb, _ = km_ref.shape
    blk_f = lax.broadcasted_iota(I32, (nb, 1), 0).astype(F32)
    row = lax.broadcasted_iota(I32, (nq, LANE), 0)
    lane = lax.broadcasted_iota(I32, (nq, LANE), 1)
    for h in range(nh):
        cols = slice(h * HEAD_DIM, (h + 1) * HEAD_DIM)
        km = km_ref[h]
        out = jnp.zeros((nq, LANE), F32)
        for qi in range(nq):
            qv = q_ref[qi:qi + 1, cols] * (HEAD_DIM ** -0.5)
            sc = jnp.sum(km * qv, axis=1, keepdims=True)
            for r in range(MOBA_TOPK):
                best = jnp.max(sc, axis=0, keepdims=True)
                first = jnp.min(jnp.where(sc == best, blk_f, float(nb)), axis=0, keepdims=True)
                out = jnp.where((row == qi) & (lane == r), first, out)
                sc = jnp.where(blk_f == first, -jnp.inf, sc)
        o_ref[h] = out.astype(I32)


def _sample_topk(q, kmean):
    db, t, d_att = q.shape
    _, nh, nb, hd = kmean.shape
    out = pl.pallas_call(
        _sample_topk_kernel,
        grid=(db,),
        in_specs=[pl.BlockSpec((None, t, d_att), lambda bi: (bi, 0, 0)),
                  pl.BlockSpec((None, nh, nb, hd), lambda bi: (bi, 0, 0, 0))],
        out_specs=pl.BlockSpec((None, nh, t, LANE), lambda bi: (bi, 0, 0, 0)),
        out_shape=jax.ShapeDtypeStruct((db, nh, t, LANE), I32),
        compiler_params=_params("parallel"),
        name="sample_topk",
    )(q, kmean)
    return out[..., :MOBA_TOPK]


def _last_page_bias_kernel(rb_ref, o_ref):
    nh, nq, page = o_ref.shape
    qi = lax.broadcasted_iota(I32, (nq, page), 0)
    row = lax.broadcasted_iota(I32, (nq, page), 1)
    for h in range(nh):
        o_ref[h] = _t5_bias(page + qi - row, rb_ref, h)


def _last_page_bias(rel_bias, nq):
    nh = rel_bias.shape[0]
    out = pl.pallas_call(
        _last_page_bias_kernel,
        in_specs=[pl.BlockSpec(memory_space=pltpu.SMEM)],
        out_shape=jax.ShapeDtypeStruct((nh, nq, PAGE_SIZE), F32),
        name="last_page_bias",
    )(rel_bias)
    return out[..., None]


class _SampleGather:
    def __init__(self, topk_ref, pt_ref, ck_hbm, cv_hbm, kbuf, vbuf, sems, *, layer, n_pages, nh, nq):
        self.topk_ref, self.pt_ref = topk_ref, pt_ref
        self.src = (ck_hbm, cv_hbm)
        self.dst = (kbuf, vbuf)
        self.sems = sems
        self.layer, self.n_pages, self.nh, self.nq = layer, n_pages, nh, nq
        self.n_sel = MOBA_TOPK * PAGES_PER_BLOCK
        self.slots = [(qi, s, p) for qi in range(nq) for s in range(MOBA_TOPK)
                      for p in range(PAGES_PER_BLOCK)]

    def chosen_block(self, g, qi, s):
        return self.topk_ref[(g * self.nq + qi) * MOBA_TOPK + s]

    def slot_index(self, qi, s, p):
        return qi * self.n_sel + s * PAGES_PER_BLOCK + p

    def _copies(self, g, half, qi, s, p):
        page = self.pt_ref[(g // self.nh) * self.n_pages
                           + self.chosen_block(g, qi, s) * PAGES_PER_BLOCK + p]
        n = self.slot_index(qi, s, p)
        head = g % self.nh
        return [pltpu.make_async_copy(src.at[self.layer, page, :, head, :], dst.at[half, n],
                                      self.sems.at[which, half, n])
                for which, (src, dst) in enumerate(zip(self.src, self.dst))]

    def start(self, g, half):
        for slot in self.slots:
            for cp in self._copies(g, half, *slot):
                cp.start()

    def wait(self, g, half):
        for slot in self.slots:
            for cp in self._copies(g, half, *slot):
                cp.wait()


def _sample_attend(gather, step, rb_ref, near_ref, q_ref, kn_ref, vn_ref, z_ref, o_ref, *, n_blocks):
    nq = gather.nq
    h = step % gather.nh
    half = step % 2
    kbuf, vbuf = gather.dst
    nrow = lax.broadcasted_iota(I32, (nq, 1), 0)
    far_bias = rb_ref[h, N_BUCKETS - 1]
    for qi in range(nq):
        qv = q_ref[qi:qi + 1, :] * (HEAD_DIM ** -0.5)
        logits = []
        for s in range(MOBA_TOPK):
            blk = gather.chosen_block(step, qi, s)
            for p in range(PAGES_PER_BLOCK):
                kp = kbuf[half, gather.slot_index(qi, s, p)]
                lg = jnp.sum(kp * qv, axis=1, keepdims=True)
                if p == PAGES_PER_BLOCK - 1:
                    bias = jnp.where(blk == n_blocks - 1, near_ref[qi], far_bias)
                else:
                    bias = far_bias
                logits.append((lg + bias, gather.slot_index(qi, s, p)))
        own = jnp.sum(kn_ref[...] * qv, axis=1, keepdims=True) + _t5_bias(qi - nrow, rb_ref, h)
        own = jnp.where(nrow <= qi, own, NEG_INF)
        m = jnp.max(own, axis=0, keepdims=True)
        for lg, _ in logits:
            m = jnp.maximum(m, jnp.max(lg, axis=0, keepdims=True))
        p_own = jnp.exp(own - m)
        l = jnp.sum(p_own, axis=0, keepdims=True)
        acc = jnp.sum(p_own * vn_ref[...], axis=0, keepdims=True)
        for lg, n in logits:
            pn = jnp.exp(lg - m)
            l = l + jnp.sum(pn, axis=0, keepdims=True)
            acc = acc + jnp.sum(pn * vbuf[half, n], axis=0, keepdims=True)
        o_ref[qi:qi + 1, :] = acc / l * _silu(z_ref[qi:qi + 1, :])


def _out_proj_step(ma_ref, mc_ref, wa_ref, wc_ref, g_ref, x_hbm, o_ref, ss_ref, x_buf, x_sem):
    i, j = pl.program_id(0), pl.program_id(1)
    tm = o_ref.shape[0]
    tn = wa_ref.shape[1]

    def x_copy():
        return pltpu.make_async_copy(x_hbm.at[pl.ds(pl.multiple_of(i * tm, tm), tm), :], x_buf, x_sem)

    @pl.when(j == 0)
    def _():
        ss_ref[...] = jnp.zeros_like(ss_ref)
        x_copy().start()

    y = (jnp.dot(ma_ref[...].astype(BF16), wa_ref[...], preferred_element_type=F32)
         + jnp.dot(mc_ref[...].astype(BF16), wc_ref[...], preferred_element_type=F32))
    ss_ref[...] += jnp.sum(y * y, axis=-1, keepdims=True)
    o_ref[:, pl.ds(pl.multiple_of(j * tn, tn), tn)] = y

    @pl.when(j == pl.num_programs(1) - 1)
    def _():
        x_copy().wait()
        inv = lax.rsqrt(ss_ref[...] / o_ref.shape[1] + RMS_EPS)
        o_ref[...] = x_buf[...] + o_ref[...] * inv * g_ref[...]


def _out_proj_kernel(*refs):
    _out_proj_step(*refs)


def _out_proj_hosting_kernel(topk_ref, pt_ref, ma_ref, mc_ref, wa_ref, wc_ref, g_ref, x_hbm,
                             rb_ref, near_ref, q_ref, kn_ref, vn_ref, z_ref, ck_hbm, cv_hbm,
                             o_ref, mix_ref, ss_ref, x_buf, x_sem, kbuf, vbuf, sems,
                             *, layer, n_pages, nh):
    step = pl.program_id(0) * pl.num_programs(1) + pl.program_id(1)
    n_steps = pl.num_programs(0) * pl.num_programs(1)
    half = step % 2
    gather = _SampleGather(topk_ref, pt_ref, ck_hbm, cv_hbm, kbuf, vbuf, sems,
                           layer=layer, n_pages=n_pages, nh=nh, nq=q_ref.shape[0])

    @pl.when(step == 0)
    def _():
        gather.start(step, half)

    @pl.when(step + 1 < n_steps)
    def _():
        gather.start(step + 1, 1 - half)

    _out_proj_step(ma_ref, mc_ref, wa_ref, wc_ref, g_ref, x_hbm, o_ref, ss_ref, x_buf, x_sem)

    gather.wait(step, half)
    _sample_attend(gather, step, rb_ref, near_ref, q_ref, kn_ref, vn_ref, z_ref, mix_ref,
                   n_blocks=n_pages // PAGES_PER_BLOCK)


def _out_proj(mix_att, mix_conv, w_out, layer, x, g, tm, tn, sample=None):
    m, d = x.shape
    ka, kc = mix_att.shape[1], mix_conv.shape[1]
    assert ka == kc
    grid = (m // tm, d // tn)
    in_specs = [pl.BlockSpec((tm, ka), lambda i, j, *_: (i, 0)),
                pl.BlockSpec((tm, kc), lambda i, j, *_: (i, 0)),
                pl.BlockSpec((None, ka, tn), lambda i, j, *_: (layer, 0, j)),
                pl.BlockSpec((None, kc, tn), lambda i, j, *_: (layer, 1, j)),
                pl.BlockSpec((1, d), lambda i, j, *_: (0, 0)),
                pl.BlockSpec(memory_space=pl.ANY)]
    out_spec = pl.BlockSpec((tm, d), lambda i, j, *_: (i, 0))
    out_shape = jax.ShapeDtypeStruct((m, d), F32)
    scratch = [pltpu.VMEM((tm, 1), F32), pltpu.VMEM((tm, d), F32), pltpu.SemaphoreType.DMA(())]
    args = (mix_att, mix_conv, w_out, w_out, g.reshape(1, d), x)
    if sample is None:
        return pl.pallas_call(
            _out_proj_kernel, grid=grid, in_specs=in_specs, out_specs=out_spec, out_shape=out_shape,
            scratch_shapes=scratch, compiler_params=_params("arbitrary", "arbitrary"),
            name="out_proj",
        )(*args)

    q, k_stack, v_stack, pr, cache_k, cache_v, page_table, topk, rel_bias, near_bias = sample
    db, t, d_att = q.shape
    nh = d_att // HEAD_DIM
    n_pages = page_table.shape[1]
    assert (n_pages * PAGE_SIZE) % MOBA_BLOCK == 0 and t <= MOBA_BLOCK
    assert grid[0] * grid[1] == db * nh, "one sample (batch, head) per grid step"
    assert PAGE_SIZE + 1 >= MAX_DISTANCE
    n_buf = t * MOBA_TOPK * PAGES_PER_BLOCK

    def head_map(i, j, *_):
        step = i * grid[1] + j
        return (step // nh, 0, step % nh)

    head = pl.BlockSpec((None, t, HEAD_DIM), head_map)
    new_rows = pl.BlockSpec((None, None, t, HEAD_DIM), lambda i, j, *_: (layer,) + head_map(i, j))
    near = pl.BlockSpec((None, t, PAGE_SIZE, 1), lambda i, j, *_: (head_map(i, j)[2], 0, 0, 0))
    hbm = pl.BlockSpec(memory_space=pl.ANY)
    return pl.pallas_call(
        functools.partial(_out_proj_hosting_kernel, layer=layer, n_pages=n_pages, nh=nh),
        grid_spec=pltpu.PrefetchScalarGridSpec(
            num_scalar_prefetch=2,
            grid=grid,
            in_specs=in_specs + [pl.BlockSpec(memory_space=pltpu.SMEM), near, head, new_rows,
                                 new_rows, head, hbm, hbm],
            out_specs=[out_spec, head],
            scratch_shapes=scratch + [pltpu.VMEM((2, n_buf, PAGE_SIZE, HEAD_DIM), F32),
                                      pltpu.VMEM((2, n_buf, PAGE_SIZE, HEAD_DIM), F32),
                                      pltpu.SemaphoreType.DMA((2, 2, n_buf))],
        ),
        out_shape=[out_shape, jax.ShapeDtypeStruct((db, t, d_att), F32)],
        compiler_params=_params("arbitrary", "arbitrary"),
        name="out_proj_sample_attn",
    )(topk.reshape(-1), page_table.reshape(-1), *args, rel_bias, near_bias, q, k_stack, v_stack, pr,
      cache_k, cache_v)


def _proj_tiles(m):
    if m % 2048 == 0:
        return (2048, 256), 512
    return (m, 1024), m


def _project_in(x, layer, kv_stack, w_in, g_pre, d_att, means=None):
    b, t, d = x.shape
    m = b * t
    (tm, tn), tm_norm = _proj_tiles(m)
    h = _rmsnorm(x.reshape(m, d), g_pre, min(tm_norm, 256))
    q = _matmul_cols(h, w_in, layer, 0, d_att, tm, tn).reshape(b, t, d_att)
    k_stack, v_stack = (
        _matmul_cols(h, w_in, layer, (n + 1) * d_att, d_att, tm, tn,
                     stack=kv_stack[n].reshape(-1, m, d_att)).reshape(-1, b, t, d_att)
        for n in range(2))
    pr = _matmul_cols(h, w_in, layer, 3 * d_att, w_in.shape[2] - 3 * d_att, tm, tn, means=means)
    pr, km = pr if means is not None else (pr, None)
    return q, (k_stack, v_stack), pr.reshape(b, t, -1), km


def kernel(x_prompt, x_sample, cache_k, cache_v, state_sc, state_cf, page_table, w_in, w_out,
           norm_pre, norm_post, sc_w, cf_w, cf_b, cf_ln_g, cf_ln_b, rel_bias):
    depth = w_in.shape[0]
    b, t, d = x_prompt.shape
    db, dt, _ = x_sample.shape
    d_sc, d_cf = sc_w.shape[2], cf_w.shape[2]
    d_att = (w_in.shape[2] - 4 * d_sc - 3 * d_cf) // 4
    nh, hd = cache_k.shape[3], cache_k.shape[4]
    w_out_b = w_out.astype(BF16)
    near_bias = _last_page_bias(rel_bias, dt)
    n_pages = page_table.shape[1]
    pages_lo = (db // 2) * n_pages
    sc_hist = lambda st: st[:, SC_HALO - (SC_WIDTH - 1):]
    cf_hist = lambda st: st[:, CF_HALO - (CF_WIDTH - 1):]
    tm_out_p, tm_out_s = _proj_tiles(b * t)[1], _proj_tiles(db * dt)[1]

    hp, hs = x_prompt, x_sample
    kv_p = (jnp.zeros((depth, b, t, d_att), F32),) * 2
    kv_s = (jnp.zeros((depth, db, dt, d_att), F32),) * 2
    outs = [[] for _ in range(4)]
    for l in range(depth):
        conv_w = (sc_w[l], cf_w[l], cf_b[l], cf_ln_g[l], cf_ln_b[l], d_att, d_sc, d_cf)

        q, kv_p, pr, km_lo = _project_in(hp, l, kv_p, w_in, norm_pre[l], d_att,
                                         means=(cache_k, page_table, 0, pages_lo))
        mix_att, km_hi = _prompt_attention(q, *kv_p, pr, rel_bias, l,
                                           (cache_k, page_table, pages_lo, db * n_pages - pages_lo))
        kmean = jnp.concatenate([km_lo, km_hi]).reshape(db, n_pages // PAGES_PER_BLOCK, nh, hd)
        mix_conv, scp, cfp = _conv_mixers(pr, jnp.zeros((b, SC_HALO, d_sc), F32),
                                          jnp.zeros((b, CF_HALO, d_cf), F32), *conv_w, tt=256)

        q_s, kv_s, pr_s, _ = _project_in(hs, l, kv_s, w_in, norm_pre[l], d_att)
        topk = _sample_topk(q_s, kmean.transpose(0, 2, 1, 3))
        hp, mix_att_s = _out_proj(
            mix_att.reshape(b * t, d_att), mix_conv.reshape(b * t, -1), w_out_b, l,
            hp.reshape(b * t, d), norm_post[l], tm_out_p, 512,
            sample=(q_s, *kv_s, pr_s, cache_k, cache_v, page_table, topk, rel_bias, near_bias))
        hp = hp.reshape(b, t, d)

        mix_conv_s, scs, cfs = _conv_mixers(pr_s, _pad_history(state_sc[l], SC_HALO),
                                            _pad_history(state_cf[l], CF_HALO), *conv_w, tt=dt)
        hs = _out_proj(mix_att_s.reshape(db * dt, d_att), mix_conv_s.reshape(db * dt, -1), w_out_b, l,
                       hs.reshape(db * dt, d), norm_post[l], tm_out_s, 512).reshape(db, dt, d)
        for lst, val in zip(outs, (sc_hist(scp), sc_hist(scs), cf_hist(cfp), cf_hist(cfs))):
            lst.append(val)

    scp, scs, cfp, cfs = (jnp.stack(o) for o in outs)
    return (hp, hs,
            kv_p[0].reshape(depth, b, t, nh, hd), kv_p[1].reshape(depth, b, t, nh, hd),
            kv_s[0].reshape(depth, db, dt, nh, hd), kv_s[1].reshape(depth, db, dt, nh, hd),
            scp, scs, cfp, cfs)
```

```python
import functools
import math

import jax
import jax.numpy as jnp
from jax import lax
from jax.experimental import pallas as pl
from jax.experimental.pallas import tpu as pltpu

F32 = jnp.float32
BF16 = jnp.bfloat16
I32 = jnp.int32

HEAD_DIM = 128
PAGE_SIZE = 128
SC_WIDTH = 3
CF_WIDTH = 31
MOBA_BLOCK = 256
MOBA_TOPK = 3
N_BUCKETS = 32
MAX_DISTANCE = 128
RMS_EPS = 1e-6
LN_EPS = 1e-5
NEG_INF = -1e30

PAGES_PER_BLOCK = MOBA_BLOCK // PAGE_SIZE
SUBLANE = 8
BF16_SUBLANE = 16
LANE = 128
SC_HALO = -(-(SC_WIDTH - 1) // SUBLANE) * SUBLANE
CF_HALO = -(-(CF_WIDTH - 1) // SUBLANE) * SUBLANE
CONV_ROWS = 128

VMEM_LIMIT = 60 * 1024 * 1024


def _params(*sem):
    return pltpu.CompilerParams(dimension_semantics=sem, vmem_limit_bytes=VMEM_LIMIT)


def _silu(x):
    return x * jax.nn.sigmoid(x)


def _rmsnorm_kernel(x_ref, g_ref, o_ref):
    x = x_ref[...]
    ms = jnp.mean(x * x, axis=-1, keepdims=True)
    o_ref[...] = (x * lax.rsqrt(ms + RMS_EPS) * g_ref[...]).astype(o_ref.dtype)


def _rmsnorm(x, g, tm):
    m, d = x.shape
    return pl.pallas_call(
        _rmsnorm_kernel,
        grid=(m // tm,),
        in_specs=[pl.BlockSpec((tm, d), lambda i: (i, 0)),
                  pl.BlockSpec((1, d), lambda i: (0, 0))],
        out_specs=pl.BlockSpec((tm, d), lambda i: (i, 0)),
        out_shape=jax.ShapeDtypeStruct((m, d), BF16),
        compiler_params=_params("parallel"),
        name="rmsnorm_pre",
    )(x, g.reshape(1, d))


def _block_means(page_refs, o_ref):
    for n in range(o_ref.shape[0]):
        total = jnp.zeros(o_ref.shape[1:], F32)
        for p in range(PAGES_PER_BLOCK):
            total = total + jnp.sum(page_refs[n * PAGES_PER_BLOCK + p][...], axis=0)
        o_ref[n] = total * (1.0 / MOBA_BLOCK)


class _BlockMeanShare:
    def __init__(self, cache, layer, first_page, n_pages, n_steps, step_of):
        _, _, page, nh, hd = cache.shape
        per_step = -(-n_pages // n_steps)
        self.pages_per_step = -(-per_step // PAGES_PER_BLOCK) * PAGES_PER_BLOCK
        assert n_pages % self.pages_per_step == 0
        self.n_host_steps = n_pages // self.pages_per_step
        self.all_steps = self.n_host_steps == n_steps
        blocks_per_step = self.pages_per_step // PAGES_PER_BLOCK
        host_step = lambda *idx: jnp.minimum(step_of(*idx), self.n_host_steps - 1)

        def page_spec(n):
            def imap(*idx):
                *grid_idx, pt_ref = idx
                return (layer, pt_ref[first_page + host_step(*grid_idx) * self.pages_per_step + n],
                        0, 0, 0)
            return pl.BlockSpec((None, None, page, nh, hd), imap)

        self.in_specs = [page_spec(n) for n in range(self.pages_per_step)]
        self.args = [cache] * self.pages_per_step
        self.out_spec = pl.BlockSpec((blocks_per_step, nh, hd),
                                     lambda *idx: (host_step(*idx[:-1]), 0, 0))
        self.out_shape = jax.ShapeDtypeStruct((n_pages // PAGES_PER_BLOCK, nh, hd), F32)

    def reduce(self, step, page_refs, o_ref):
        if self.all_steps:
            _block_means(page_refs, o_ref)
        else:
            pl.when(step < self.n_host_steps)(lambda: _block_means(page_refs, o_ref))


def _matmul_kernel(a_ref, w_ref, s_ref, *rest):
    o_ref, os_ref = rest[-2:]
    wb = w_ref[...].astype(BF16)
    o_ref[...] = jnp.dot(a_ref[...], wb, preferred_element_type=F32)

    @pl.when(pl.program_id(0) == 0)
    def _():
        os_ref[...] = jnp.dot(s_ref[...], wb, preferred_element_type=F32)


def _matmul_cols(a, small, w, layer, col0, ncols, tm, tn, stacks=None):
    m, k = a.shape
    ms = small.shape[0]
    c0 = col0 // tn
    grid = (m // tm, ncols // tn)
    small_col = lambda i, j: jnp.where(i == 0, j, grid[1] - 1)
    in_specs = [pl.BlockSpec((tm, k), lambda i, j: (i, 0)),
                pl.BlockSpec((None, k, tn), lambda i, j: (layer, 0, c0 + j)),
                pl.BlockSpec((ms, k), lambda i, j: (0, 0))]
    args = [a, w, small]
    aliases = {}
    if stacks is None:
        out_specs = [pl.BlockSpec((tm, tn), lambda i, j: (i, j)),
                     pl.BlockSpec((ms, tn), lambda i, j: (0, small_col(i, j)))]
        out_shape = [jax.ShapeDtypeStruct((m, ncols), F32), jax.ShapeDtypeStruct((ms, ncols), F32)]
    else:
        out_specs = [pl.BlockSpec((None, tm, tn), lambda i, j: (layer, i, j)),
                     pl.BlockSpec((None, ms, tn), lambda i, j: (layer, 0, small_col(i, j)))]
        out_shape = [jax.ShapeDtypeStruct(buf.shape, F32) for buf in stacks]
        in_specs += [pl.BlockSpec(memory_space=pl.ANY)] * 2
        args += list(stacks)
        aliases = {3: 0, 4: 1}
    return pl.pallas_call(
        _matmul_kernel,
        grid=grid,
        in_specs=in_specs,
        out_specs=out_specs,
        out_shape=out_shape,
        input_output_aliases=aliases,
        compiler_params=_params("arbitrary", "arbitrary"),
        name="in_proj",
    )(*args)


def _t5_bias(dist, rb_ref, h):
    max_exact = N_BUCKETS // 2
    d = jnp.maximum(dist, 0)
    ratio = jnp.log(jnp.maximum(d, 1).astype(F32) / max_exact) / math.log(MAX_DISTANCE / max_exact)
    large = max_exact + (ratio * (N_BUCKETS - max_exact)).astype(I32)
    large = jnp.minimum(large, N_BUCKETS - 1)
    bucket = jnp.where(d < max_exact, d, large)
    bias = jnp.zeros(dist.shape, F32)
    for b in range(N_BUCKETS):
        bias = jnp.where(bucket == b, rb_ref[h, b], bias)
    return bias


def _split_bf16(x):
    hi = x.astype(BF16)
    lo = (x - hi.astype(F32)).astype(BF16)
    return hi, lo


_NT = (((1,), (1,)), ((), ()))


def _prompt_attn_kernel(pt_ref, rb_ref, q_ref, k_ref, v_ref, z_ref, *refs, share):
    qb_ref, kb_ref, vt_ref, sel_ref, bias_ref, s_ref, p_ref = refs[-7:]
    o_ref, km_ref = refs[-9:-7]
    h, b = pl.program_id(0), pl.program_id(1)
    share.reduce(h * pl.num_programs(1) + b, refs[:-9], km_ref)

    t = q_ref.shape[0]
    nb = t // MOBA_BLOCK
    blk = MOBA_BLOCK
    key = lax.broadcasted_iota(I32, (blk, blk), 0)
    qry = lax.broadcasted_iota(I32, (blk, blk), 1)

    @pl.when(b == 0)
    def _():
        bias_ref[0] = _t5_bias(qry - key, rb_ref, h)
        bias_ref[1] = _t5_bias(qry - key + blk, rb_ref, h)
    far_bias = rb_ref[h, N_BUCKETS - 1]

    kf = k_ref[...]
    qs = q_ref[...] * (HEAD_DIM ** -0.5)
    qh, ql = _split_bf16(qs)
    qb_ref[...] = qh
    kb_ref[...] = kf.astype(BF16)
    for j in range(nb):
        cols = slice(j * blk, (j + 1) * blk)
        vt_ref[:, cols] = v_ref[cols, :].T.astype(BF16)

    kmean = jnp.mean(kf.reshape(nb, blk, HEAD_DIM), axis=1)
    nbp = sel_ref.shape[0]
    kmean = jnp.concatenate([kmean, jnp.zeros((nbp - nb, HEAD_DIM), F32)], axis=0)
    kh, kl = _split_bf16(kmean)
    score = (lax.dot_general(kh, qh, _NT, preferred_element_type=F32)
             + lax.dot_general(kl, qh, _NT, preferred_element_type=F32)
             + lax.dot_general(kh, ql, _NT, preferred_element_type=F32))
    blk_id = lax.broadcasted_iota(I32, (nbp, t), 0)
    own = lax.broadcasted_iota(I32, (nbp, t), 1) // blk
    past = blk_id < own
    blk_f = blk_id.astype(F32)
    sc = jnp.where(past, score, NEG_INF)
    picked = jnp.zeros((nbp, t), F32)
    for _ in range(min(MOBA_TOPK, nb)):
        best = jnp.max(sc, axis=0, keepdims=True)
        first = jnp.min(jnp.where(sc == best, blk_f, float(nbp)), axis=0, keepdims=True)
        hit = blk_f == first
        picked = jnp.where(hit, 1.0, picked)
        sc = jnp.where(hit, -jnp.inf, sc)
    sel_ref[...] = jnp.where(past, picked, 0.0)

    for i in range(nb):
        rows = slice(i * blk, (i + 1) * blk)
        qi = qb_ref[rows, :]
        m = None
        for j in range(i + 1):
            cols = slice(j * blk, (j + 1) * blk)
            s = lax.dot_general(kb_ref[cols, :], qi, _NT, preferred_element_type=F32)
            if j == i:
                s = jnp.where(key <= qry, s + bias_ref[0], NEG_INF)
            else:
                s = s + (bias_ref[1] if j == i - 1 else far_bias)
                s = jnp.where(sel_ref[j:j + 1, rows] > 0.0, s, NEG_INF)
            s_ref[cols, :] = s
            mj = jnp.max(s, axis=0, keepdims=True)
            m = mj if m is None else jnp.maximum(m, mj)
        l = jnp.zeros((1, blk), F32)
        for j in range(i + 1):
            cols = slice(j * blk, (j + 1) * blk)
            p = jnp.exp(s_ref[cols, :] - m)
            l = l + jnp.sum(p, axis=0, keepdims=True)
            p_ref[cols, :] = p.astype(BF16)
        n = (i + 1) * blk
        acc = jnp.dot(vt_ref[:, 0:n], p_ref[0:n, :], preferred_element_type=F32)
        o_ref[rows, :] = ((acc / l).T * _silu(z_ref[rows, :])).astype(o_ref.dtype)


def _prompt_attention(q, k_stack, v_stack, pr, rel_bias, layer, means):
    b, t, d_att = q.shape
    nh = d_att // HEAD_DIM
    nb = t // MOBA_BLOCK
    nbp = -(-nb // BF16_SUBLANE) * BF16_SUBLANE
    cache, page_table, first_page, n_pages = means
    share = _BlockMeanShare(cache, layer, first_page, n_pages, nh * b, lambda hi, bi: hi * b + bi)

    spec = pl.BlockSpec((None, t, HEAD_DIM), lambda hi, bi, pt_ref: (bi, 0, hi))
    kv_spec = pl.BlockSpec((None, None, t, HEAD_DIM), lambda hi, bi, pt_ref: (layer, bi, 0, hi))
    return pl.pallas_call(
        functools.partial(_prompt_attn_kernel, share=share),
        grid_spec=pltpu.PrefetchScalarGridSpec(
            num_scalar_prefetch=1,
            grid=(nh, b),
            in_specs=[pl.BlockSpec(memory_space=pltpu.SMEM), spec, kv_spec, kv_spec, spec]
                     + share.in_specs,
            out_specs=[spec, share.out_spec],
            scratch_shapes=[pltpu.VMEM((t, HEAD_DIM), BF16),
                            pltpu.VMEM((t, HEAD_DIM), BF16),
                            pltpu.VMEM((HEAD_DIM, t), BF16),
                            pltpu.VMEM((nbp, t), F32),
                            pltpu.VMEM((2, MOBA_BLOCK, MOBA_BLOCK), F32),
                            pltpu.VMEM((t, MOBA_BLOCK), F32),
                            pltpu.VMEM((t, MOBA_BLOCK), BF16)],
        ),
        out_shape=[jax.ShapeDtypeStruct((b, t, d_att), BF16), share.out_shape],
        compiler_params=_params("arbitrary", "arbitrary"),
        name="prompt_attn",
    )(page_table.reshape(-1), rel_bias, q, k_stack, v_stack, pr, *share.args)


def _conv_kernel(scw_ref, cfw_ref, cfb_ref, lng_ref, lnb_ref, psc_ref, pcf_ref,
                 sc_in_ref, sc_b_ref, sc_c_ref, z_sc_ref, cf_a_ref, cf_g_ref, z_cf_ref,
                 mix_ref, sc_state_ref, cf_state_ref, u_ext, g_ext, y_scr):
    tt, d_sc = sc_in_ref.shape
    d_cf = cf_a_ref.shape[1]

    @pl.when(pl.program_id(1) == 0)
    def _():
        g_ext[...] = jnp.zeros_like(g_ext)
        u_ext[0:SC_HALO, :] = psc_ref[...]
        g_ext[0:CF_HALO, :] = pcf_ref[...]

    u_ext[SC_HALO:SC_HALO + tt, :] = sc_c_ref[...] * sc_in_ref[...]
    g_ext[CF_HALO:CF_HALO + tt, :] = cf_a_ref[...] * jax.nn.sigmoid(cf_g_ref[...])

    sc_y = jnp.zeros((tt, d_sc), F32)
    for kk in range(SC_WIDTH):
        off = SC_HALO - (SC_WIDTH - 1) + kk
        sc_y = sc_y + scw_ref[kk:kk + 1, :] * u_ext[off:off + tt, :]
    mix_ref[:, 0:d_sc] = (sc_b_ref[...] * sc_y * _silu(z_sc_ref[...])).astype(mix_ref.dtype)

    first = CF_HALO - (CF_WIDTH - 1)
    sub = min(tt, CONV_ROWS)
    for c in range(d_cf // LANE):
        lanes = slice(c * LANE, (c + 1) * LANE)
        for r0 in range(0, tt, sub):
            acc = jnp.zeros((sub, LANE), F32)
            for rem in range(SUBLANE):
                part = None
                for base in range(0, CF_HALO + SUBLANE, SUBLANE):
                    kk = base + rem - first
                    if 0 <= kk < CF_WIDTH:
                        term = (cfw_ref[kk:kk + 1, lanes]
                                * g_ext[r0 + base:r0 + base + sub + SUBLANE, lanes])
                        part = term if part is None else part + term
                if part is not None:
                    acc = acc + part[rem:rem + sub, :]
            y_scr[r0:r0 + sub, lanes] = acc + cfb_ref[:, lanes]
    y = y_scr[...]
    mu = jnp.mean(y, axis=-1, keepdims=True)
    var = jnp.mean(jnp.square(y - mu), axis=-1, keepdims=True)
    ln = (y - mu) * lax.rsqrt(var + LN_EPS) * lng_ref[...] + lnb_ref[...]
    mix_ref[:, d_sc:d_sc + d_cf] = (_silu(ln) * _silu(z_cf_ref[...])).astype(mix_ref.dtype)

    sc_tail = u_ext[tt:tt + SC_HALO, :]
    cf_tail = g_ext[tt:tt + CF_HALO, :]
    sc_state_ref[...] = sc_tail
    cf_state_ref[...] = cf_tail
    u_ext[0:SC_HALO, :] = sc_tail
    g_ext[0:CF_HALO, :] = cf_tail


def _conv_mixers(pr, prev_sc, prev_cf, sc_w, cf_w, cf_b, ln_g, ln_b, d_att, d_sc, d_cf, tt):
    b, t, _ = pr.shape
    assert d_sc == d_cf and tt % min(tt, CONV_ROWS) == 0
    c0 = d_att // d_sc
    col = lambda n: pl.BlockSpec((None, tt, d_sc), lambda bi, ti, n=n: (bi, ti, c0 + n))
    full = lambda a: pl.BlockSpec(a.shape, lambda bi, ti: (0,) * a.ndim)
    state = lambda rows, d: pl.BlockSpec((None, rows, d), lambda bi, ti: (bi, 0, 0))
    cf_b, ln_g, ln_b = (a.reshape(1, d_cf) for a in (cf_b, ln_g, ln_b))
    ext_rows = lambda halo: -(-(halo + tt) // SUBLANE) * SUBLANE + SUBLANE
    return pl.pallas_call(
        _conv_kernel,
        grid=(b, t // tt),
        in_specs=[full(sc_w), full(cf_w), full(cf_b), full(ln_g), full(ln_b),
                  state(SC_HALO, d_sc), state(CF_HALO, d_cf)] + [col(n) for n in range(7)],
        out_specs=[pl.BlockSpec((None, tt, d_sc + d_cf), lambda bi, ti: (bi, ti, 0)),
                   state(SC_HALO, d_sc), state(CF_HALO, d_cf)],
        out_shape=[jax.ShapeDtypeStruct((b, t, d_sc + d_cf), BF16),
                   jax.ShapeDtypeStruct((b, SC_HALO, d_sc), F32),
                   jax.ShapeDtypeStruct((b, CF_HALO, d_cf), F32)],
        scratch_shapes=[pltpu.VMEM((ext_rows(SC_HALO), d_sc), F32),
                        pltpu.VMEM((ext_rows(CF_HALO), d_cf), F32),
                        pltpu.VMEM((tt, d_cf), F32)],
        compiler_params=_params("parallel", "arbitrary"),
        name="conv_mixers",
    )(sc_w, cf_w, cf_b, ln_g, ln_b, prev_sc, prev_cf, *([pr] * 7))


def _pad_history(state, halo):
    return jnp.pad(state, ((0, 0), (halo - state.shape[1], 0), (0, 0)))


def _sample_topk_kernel(q_ref, km_ref, o_ref):
    nq = q_ref.shape[0]
    nh, nb, _ = km_ref.shape
    blk_f = lax.broadcasted_iota(I32, (nb, 1), 0).astype(F32)
    row = lax.broadcasted_iota(I32, (nq, LANE), 0)
    lane = lax.broadcasted_iota(I32, (nq, LANE), 1)
    for h in range(nh):
        cols = slice(h * HEAD_DIM, (h + 1) * HEAD_DIM)
        km = km_ref[h]
        out = jnp.zeros((nq, LANE), F32)
        for qi in range(nq):
            qv = q_ref[qi:qi + 1, cols] * (HEAD_DIM ** -0.5)
            sc = jnp.sum(km * qv, axis=1, keepdims=True)
            for r in range(MOBA_TOPK):
                best = jnp.max(sc, axis=0, keepdims=True)
                first = jnp.min(jnp.where(sc == best, blk_f, float(nb)), axis=0, keepdims=True)
                out = jnp.where((row == qi) & (lane == r), first, out)
                sc = jnp.where(blk_f == first, -jnp.inf, sc)
        o_ref[h] = out.astype(I32)


def _sample_topk(q, kmean):
    db, t, d_att = q.shape
    _, nh, nb, hd = kmean.shape
    out = pl.pallas_call(
        _sample_topk_kernel,
        grid=(db,),
        in_specs=[pl.BlockSpec((None, t, d_att), lambda bi: (bi, 0, 0)),
                  pl.BlockSpec((None, nh, nb, hd), lambda bi: (bi, 0, 0, 0))],
        out_specs=pl.BlockSpec((None, nh, t, LANE), lambda bi: (bi, 0, 0, 0)),
        out_shape=jax.ShapeDtypeStruct((db, nh, t, LANE), I32),
        compiler_params=_params("parallel"),
        name="sample_topk",
    )(q, kmean)
    return out[..., :MOBA_TOPK]


def _last_page_bias_kernel(rb_ref, o_ref):
    nh, nq, page = o_ref.shape
    qi = lax.broadcasted_iota(I32, (nq, page), 0)
    row = lax.broadcasted_iota(I32, (nq, page), 1)
    for h in range(nh):
        o_ref[h] = _t5_bias(page + qi - row, rb_ref, h)


def _last_page_bias(rel_bias, nq):
    nh = rel_bias.shape[0]
    out = pl.pallas_call(
        _last_page_bias_kernel,
        in_specs=[pl.BlockSpec(memory_space=pltpu.SMEM)],
        out_shape=jax.ShapeDtypeStruct((nh, nq, PAGE_SIZE), F32),
        name="last_page_bias",
    )(rel_bias)
    return out[..., None]


class _SampleGather:
    def __init__(self, topk_ref, pt_ref, ck_hbm, cv_hbm, kbuf, vbuf, sems, *, layer, n_pages, nh, nq):
        self.topk_ref, self.pt_ref = topk_ref, pt_ref
        self.src = (ck_hbm, cv_hbm)
        self.dst = (kbuf, vbuf)
        self.sems = sems
        self.layer, self.n_pages, self.nh, self.nq = layer, n_pages, nh, nq
        self.n_sel = MOBA_TOPK * PAGES_PER_BLOCK
        self.slots = [(qi, s, p) for qi in range(nq) for s in range(MOBA_TOPK)
                      for p in range(PAGES_PER_BLOCK)]

    def chosen_block(self, g, qi, s):
        return self.topk_ref[(g * self.nq + qi) * MOBA_TOPK + s]

    def slot_index(self, qi, s, p):
        return qi * self.n_sel + s * PAGES_PER_BLOCK + p

    def _copies(self, g, half, qi, s, p):
        page = self.pt_ref[(g // self.nh) * self.n_pages
                           + self.chosen_block(g, qi, s) * PAGES_PER_BLOCK + p]
        n = self.slot_index(qi, s, p)
        head = g % self.nh
        return [pltpu.make_async_copy(src.at[self.layer, page, :, head, :], dst.at[half, n],
                                      self.sems.at[which, half, n])
                for which, (src, dst) in enumerate(zip(self.src, self.dst))]

    def start(self, g, half):
        for slot in self.slots:
            for cp in self._copies(g, half, *slot):
                cp.start()

    def wait(self, g, half):
        for slot in self.slots:
            for cp in self._copies(g, half, *slot):
                cp.wait()


def _sample_attend(gather, step, rb_ref, near_ref, q_ref, kn_ref, vn_ref, z_ref, o_ref, *, n_blocks):
    nq = gather.nq
    h = step % gather.nh
    half = step % 2
    kbuf, vbuf = gather.dst
    nrow = lax.broadcasted_iota(I32, (nq, 1), 0)
    far_bias = rb_ref[h, N_BUCKETS - 1]
    for qi in range(nq):
        qv = q_ref[qi:qi + 1, :] * (HEAD_DIM ** -0.5)
        logits = []
        for s in range(MOBA_TOPK):
            blk = gather.chosen_block(step, qi, s)
            for p in range(PAGES_PER_BLOCK):
                kp = kbuf[half, gather.slot_index(qi, s, p)]
                lg = jnp.sum(kp * qv, axis=1, keepdims=True)
                if p == PAGES_PER_BLOCK - 1:
                    bias = jnp.where(blk == n_blocks - 1, near_ref[qi], far_bias)
                else:
                    bias = far_bias
                logits.append((lg + bias, gather.slot_index(qi, s, p)))
        own = jnp.sum(kn_ref[...] * qv, axis=1, keepdims=True) + _t5_bias(qi - nrow, rb_ref, h)
        own = jnp.where(nrow <= qi, own, NEG_INF)
        m = jnp.max(own, axis=0, keepdims=True)
        for lg, _ in logits:
            m = jnp.maximum(m, jnp.max(lg, axis=0, keepdims=True))
        p_own = jnp.exp(own - m)
        l = jnp.sum(p_own, axis=0, keepdims=True)
        acc = jnp.sum(p_own * vn_ref[...], axis=0, keepdims=True)
        for lg, n in logits:
            pn = jnp.exp(lg - m)
            l = l + jnp.sum(pn, axis=0, keepdims=True)
            acc = acc + jnp.sum(pn * vbuf[half, n], axis=0, keepdims=True)
        o_ref[qi:qi + 1, :] = acc / l * _silu(z_ref[qi:qi + 1, :])


def _out_proj_step(ma_ref, mc_ref, wa_ref, wc_ref, g_ref, x_hbm, o_ref, ss_ref, x_buf, x_sem):
    i, j = pl.program_id(0), pl.program_id(1)
    tm = o_ref.shape[0]
    tn = wa_ref.shape[1]

    def x_copy():
        return pltpu.make_async_copy(x_hbm.at[pl.ds(pl.multiple_of(i * tm, tm), tm), :], x_buf, x_sem)

    @pl.when(j == 0)
    def _():
        ss_ref[...] = jnp.zeros_like(ss_ref)
        x_copy().start()

    y = (jnp.dot(ma_ref[...].astype(BF16), wa_ref[...], preferred_element_type=F32)
         + jnp.dot(mc_ref[...].astype(BF16), wc_ref[...], preferred_element_type=F32))
    ss_ref[...] += jnp.sum(y * y, axis=-1, keepdims=True)
    o_ref[:, pl.ds(pl.multiple_of(j * tn, tn), tn)] = y

    @pl.when(j == pl.num_programs(1) - 1)
    def _():
        x_copy().wait()
        inv = lax.rsqrt(ss_ref[...] / o_ref.shape[1] + RMS_EPS)
        o_ref[...] = x_buf[...] + o_ref[...] * inv * g_ref[...]


def _out_proj_kernel(*refs):
    _out_proj_step(*refs)


def _out_proj_hosting_kernel(topk_ref, pt_ref, ma_ref, mc_ref, wa_ref, wc_ref, g_ref, x_hbm,
                             rb_ref, near_ref, q_ref, kn_ref, vn_ref, z_ref, ck_hbm, cv_hbm,
                             o_ref, mix_ref, ss_ref, x_buf, x_sem, kbuf, vbuf, sems,
                             *, layer, n_pages, nh):
    step = pl.program_id(0) * pl.num_programs(1) + pl.program_id(1)
    n_steps = pl.num_programs(0) * pl.num_programs(1)
    half = step % 2
    gather = _SampleGather(topk_ref, pt_ref, ck_hbm, cv_hbm, kbuf, vbuf, sems,
                           layer=layer, n_pages=n_pages, nh=nh, nq=q_ref.shape[0])

    @pl.when(step == 0)
    def _():
        gather.start(step, half)

    @pl.when(step + 1 < n_steps)
    def _():
        gather.start(step + 1, 1 - half)

    _out_proj_step(ma_ref, mc_ref, wa_ref, wc_ref, g_ref, x_hbm, o_ref, ss_ref, x_buf, x_sem)

    gather.wait(step, half)
    _sample_attend(gather, step, rb_ref, near_ref, q_ref, kn_ref, vn_ref, z_ref, mix_ref,
                   n_blocks=n_pages // PAGES_PER_BLOCK)


def _out_proj(mix_att, mix_conv, w_out, layer, x, g, tm, tn, sample=None):
    m, d = x.shape
    ka, kc = mix_att.shape[1], mix_conv.shape[1]
    assert ka == kc
    grid = (m // tm, d // tn)
    in_specs = [pl.BlockSpec((tm, ka), lambda i, j, *_: (i, 0)),
                pl.BlockSpec((tm, kc), lambda i, j, *_: (i, 0)),
                pl.BlockSpec((None, ka, tn), lambda i, j, *_: (layer, 0, j)),
                pl.BlockSpec((None, kc, tn), lambda i, j, *_: (layer, 1, j)),
                pl.BlockSpec((1, d), lambda i, j, *_: (0, 0)),
                pl.BlockSpec(memory_space=pl.ANY)]
    out_spec = pl.BlockSpec((tm, d), lambda i, j, *_: (i, 0))
    out_shape = jax.ShapeDtypeStruct((m, d), F32)
    scratch = [pltpu.VMEM((tm, 1), F32), pltpu.VMEM((tm, d), F32), pltpu.SemaphoreType.DMA(())]
    args = (mix_att, mix_conv, w_out, w_out, g.reshape(1, d), x)
    if sample is None:
        return pl.pallas_call(
            _out_proj_kernel, grid=grid, in_specs=in_specs, out_specs=out_spec, out_shape=out_shape,
            scratch_shapes=scratch, compiler_params=_params("arbitrary", "arbitrary"),
            name="out_proj",
        )(*args)

    q, k_stack, v_stack, pr, cache_k, cache_v, page_table, topk, rel_bias, near_bias = sample
    db, t, d_att = q.shape
    nh = d_att // HEAD_DIM
    n_pages = page_table.shape[1]
    assert (n_pages * PAGE_SIZE) % MOBA_BLOCK == 0 and t <= MOBA_BLOCK
    assert grid[0] * grid[1] == db * nh, "one sample (batch, head) per grid step"
    assert PAGE_SIZE + 1 >= MAX_DISTANCE
    n_buf = t * MOBA_TOPK * PAGES_PER_BLOCK

    def head_map(i, j, *_):
        step = i * grid[1] + j
        return (step // nh, 0, step % nh)

    head = pl.BlockSpec((None, t, HEAD_DIM), head_map)
    new_rows = pl.BlockSpec((None, None, t, HEAD_DIM), lambda i, j, *_: (layer,) + head_map(i, j))
    near = pl.BlockSpec((None, t, PAGE_SIZE, 1), lambda i, j, *_: (head_map(i, j)[2], 0, 0, 0))
    hbm = pl.BlockSpec(memory_space=pl.ANY)
    return pl.pallas_call(
        functools.partial(_out_proj_hosting_kernel, layer=layer, n_pages=n_pages, nh=nh),
        grid_spec=pltpu.PrefetchScalarGridSpec(
            num_scalar_prefetch=2,
            grid=grid,
            in_specs=in_specs + [pl.BlockSpec(memory_space=pltpu.SMEM), near, head, new_rows,
                                 new_rows, head, hbm, hbm],
            out_specs=[out_spec, head],
            scratch_shapes=scratch + [pltpu.VMEM((2, n_buf, PAGE_SIZE, HEAD_DIM), F32),
                                      pltpu.VMEM((2, n_buf, PAGE_SIZE, HEAD_DIM), F32),
                                      pltpu.SemaphoreType.DMA((2, 2, n_buf))],
        ),
        out_shape=[out_shape, jax.ShapeDtypeStruct((db, t, d_att), F32)],
        compiler_params=_params("arbitrary", "arbitrary"),
        name="out_proj_sample_attn",
    )(topk.reshape(-1), page_table.reshape(-1), *args, rel_bias, near_bias, q, k_stack, v_stack, pr,
      cache_k, cache_v)


IN_PROJ_TM, IN_PROJ_TN = 2048, 256
OUT_PROJ_TM, OUT_PROJ_TN = 512, 512
NORM_TM = 256


def _project_in(xp, xs, layer, kv_p, kv_s, w_in, g_pre, d_att):
    d = xp.shape[-1]
    shp, shs = xp.shape[:2], xs.shape[:2]
    mp, ms = shp[0] * shp[1], shs[0] * shs[1]
    hp = _rmsnorm(xp.reshape(mp, d), g_pre, NORM_TM)
    hs = _rmsnorm(xs.reshape(ms, d), g_pre, ms)
    proj = functools.partial(_matmul_cols, hp, hs, w_in, layer, tm=IN_PROJ_TM, tn=IN_PROJ_TN)
    q_p, q_s = proj(col0=0, ncols=d_att)
    kv = [proj(col0=(n + 1) * d_att, ncols=d_att,
               stacks=(kv_p[n].reshape(-1, mp, d_att), kv_s[n].reshape(-1, ms, d_att)))
          for n in range(2)]
    pr_p, pr_s = proj(col0=3 * d_att, ncols=w_in.shape[2] - 3 * d_att)
    group = lambda sh, q, n, pr: (q.reshape(*sh, d_att),
                                  tuple(kv[i][n].reshape(-1, *sh, d_att) for i in range(2)),
                                  pr.reshape(*sh, -1))
    return group(shp, q_p, 0, pr_p), group(shs, q_s, 1, pr_s)


def kernel(x_prompt, x_sample, cache_k, cache_v, state_sc, state_cf, page_table, w_in, w_out,
           norm_pre, norm_post, sc_w, cf_w, cf_b, cf_ln_g, cf_ln_b, rel_bias):
    depth = w_in.shape[0]
    b, t, d = x_prompt.shape
    db, dt, _ = x_sample.shape
    d_sc, d_cf = sc_w.shape[2], cf_w.shape[2]
    d_att = (w_in.shape[2] - 4 * d_sc - 3 * d_cf) // 4
    nh, hd = cache_k.shape[3], cache_k.shape[4]
    w_out_b = w_out.astype(BF16)
    near_bias = _last_page_bias(rel_bias, dt)
    n_pages = page_table.shape[1]
    sc_hist = lambda st: st[:, SC_HALO - (SC_WIDTH - 1):]
    cf_hist = lambda st: st[:, CF_HALO - (CF_WIDTH - 1):]

    hp, hs = x_prompt, x_sample
    kv_p = (jnp.zeros((depth, b, t, d_att), F32),) * 2
    kv_s = (jnp.zeros((depth, db, dt, d_att), F32),) * 2
    outs = [[] for _ in range(4)]
    for l in range(depth):
        conv_w = (sc_w[l], cf_w[l], cf_b[l], cf_ln_g[l], cf_ln_b[l], d_att, d_sc, d_cf)
        (q, kv_p, pr), (q_s, kv_s, pr_s) = _project_in(hp, hs, l, kv_p, kv_s, w_in, norm_pre[l], d_att)

        mix_att, kmean = _prompt_attention(q, *kv_p, pr, rel_bias, l,
                                           (cache_k, page_table, 0, db * n_pages))
        kmean = kmean.reshape(db, n_pages // PAGES_PER_BLOCK, nh, hd)
        mix_conv, scp, cfp = _conv_mixers(pr, jnp.zeros((b, SC_HALO, d_sc), F32),
                                          jnp.zeros((b, CF_HALO, d_cf), F32), *conv_w, tt=256)

        topk = _sample_topk(q_s, kmean.transpose(0, 2, 1, 3))
        hp, mix_att_s = _out_proj(
            mix_att.reshape(b * t, d_att), mix_conv.reshape(b * t, -1), w_out_b, l,
            hp.reshape(b * t, d), norm_post[l], OUT_PROJ_TM, OUT_PROJ_TN,
            sample=(q_s, *kv_s, pr_s, cache_k, cache_v, page_table, topk, rel_bias, near_bias))
        hp = hp.reshape(b, t, d)

        mix_conv_s, scs, cfs = _conv_mixers(pr_s, _pad_history(state_sc[l], SC_HALO),
                                            _pad_history(state_cf[l], CF_HALO), *conv_w, tt=dt)
        hs = _out_proj(mix_att_s.reshape(db * dt, d_att), mix_conv_s.reshape(db * dt, -1), w_out_b, l,
                       hs.reshape(db * dt, d), norm_post[l], db * dt, OUT_PROJ_TN).reshape(db, dt, d)
        for lst, val in zip(outs, (sc_hist(scp), sc_hist(scs), cf_hist(cfp), cf_hist(cfs))):
            lst.append(val)

    scp, scs, cfp, cfs = (jnp.stack(o) for o in outs)
    return (hp, hs,
            kv_p[0].reshape(depth, b, t, nh, hd), kv_p[1].reshape(depth, b, t, nh, hd),
            kv_s[0].reshape(depth, db, dt, nh, hd), kv_s[1].reshape(depth, db, dt, nh, hd),
            scp, scs, cfp, cfs)
```

```python
import functools
import math

import jax
import jax.numpy as jnp
from jax import lax
from jax.experimental import pallas as pl
from jax.experimental.pallas import tpu as pltpu

F32 = jnp.float32
BF16 = jnp.bfloat16
I32 = jnp.int32

HEAD_DIM = 128
PAGE_SIZE = 128
SC_WIDTH = 3
CF_WIDTH = 31
MOBA_BLOCK = 256
MOBA_TOPK = 3
N_BUCKETS = 32
MAX_DISTANCE = 128
RMS_EPS = 1e-6
LN_EPS = 1e-5
NEG_INF = -1e30

PAGES_PER_BLOCK = MOBA_BLOCK // PAGE_SIZE
SUBLANE = 8
BF16_SUBLANE = 16
LANE = 128
SC_HALO = -(-(SC_WIDTH - 1) // SUBLANE) * SUBLANE
CF_HALO = -(-(CF_WIDTH - 1) // SUBLANE) * SUBLANE
CONV_ROWS = 128

VMEM_LIMIT = 60 * 1024 * 1024


def _params(*sem):
    return pltpu.CompilerParams(dimension_semantics=sem, vmem_limit_bytes=VMEM_LIMIT)


def _silu(x):
    return x * jax.nn.sigmoid(x)


def _rmsnorm_kernel(x_ref, g_ref, o_ref):
    x = x_ref[...]
    ms = jnp.mean(x * x, axis=-1, keepdims=True)
    o_ref[...] = (x * lax.rsqrt(ms + RMS_EPS) * g_ref[...]).astype(o_ref.dtype)


def _rmsnorm(x, g, tm):
    m, d = x.shape
    return pl.pallas_call(
        _rmsnorm_kernel,
        grid=(m // tm,),
        in_specs=[pl.BlockSpec((tm, d), lambda i: (i, 0)),
                  pl.BlockSpec((1, d), lambda i: (0, 0))],
        out_specs=pl.BlockSpec((tm, d), lambda i: (i, 0)),
        out_shape=jax.ShapeDtypeStruct((m, d), BF16),
        compiler_params=_params("parallel"),
        name="rmsnorm_pre",
    )(x, g.reshape(1, d))


def _block_means(page_refs, o_ref):
    for n in range(o_ref.shape[0]):
        total = jnp.zeros(o_ref.shape[1:], F32)
        for p in range(PAGES_PER_BLOCK):
            total = total + jnp.sum(page_refs[n * PAGES_PER_BLOCK + p][...], axis=0)
        o_ref[n] = total * (1.0 / MOBA_BLOCK)


class _BlockMeanShare:
    def __init__(self, cache, layer, first_page, n_pages, n_steps, step_of):
        _, _, page, nh, hd = cache.shape
        per_step = -(-n_pages // n_steps)
        self.pages_per_step = -(-per_step // PAGES_PER_BLOCK) * PAGES_PER_BLOCK
        assert n_pages % self.pages_per_step == 0
        self.n_host_steps = n_pages // self.pages_per_step
        self.all_steps = self.n_host_steps == n_steps
        blocks_per_step = self.pages_per_step // PAGES_PER_BLOCK
        host_step = lambda *idx: jnp.minimum(step_of(*idx), self.n_host_steps - 1)

        def page_spec(n):
            def imap(*idx):
                *grid_idx, pt_ref = idx
                return (layer, pt_ref[first_page + host_step(*grid_idx) * self.pages_per_step + n],
                        0, 0, 0)
            return pl.BlockSpec((None, None, page, nh, hd), imap)

        self.in_specs = [page_spec(n) for n in range(self.pages_per_step)]
        self.args = [cache] * self.pages_per_step
        self.out_spec = pl.BlockSpec((blocks_per_step, nh, hd),
                                     lambda *idx: (host_step(*idx[:-1]), 0, 0))
        self.out_shape = jax.ShapeDtypeStruct((n_pages // PAGES_PER_BLOCK, nh, hd), F32)

    def reduce(self, step, page_refs, o_ref):
        if self.all_steps:
            _block_means(page_refs, o_ref)
        else:
            pl.when(step < self.n_host_steps)(lambda: _block_means(page_refs, o_ref))


def _in_tile_ranges(j, ranges):
    hit = False
    for first, end in ranges:
        hit = jnp.logical_or(hit, jnp.logical_and(j >= first, j < end))
    return hit


def _matmul_kernel(a_ref, w_ref, s_ref, *rest, silu_tiles, sigmoid_tiles):
    o_ref, os_ref = rest[-2:]
    j = pl.program_id(1)
    wb = w_ref[...].astype(BF16)

    def finish(y):
        if not (silu_tiles or sigmoid_tiles):
            return y
        sg = jax.nn.sigmoid(y)
        y = jnp.where(_in_tile_ranges(j, sigmoid_tiles), sg, y) if sigmoid_tiles else y
        return jnp.where(_in_tile_ranges(j, silu_tiles), y * sg, y) if silu_tiles else y

    o_ref[...] = finish(jnp.dot(a_ref[...], wb, preferred_element_type=F32))

    @pl.when(pl.program_id(0) == 0)
    def _():
        os_ref[...] = finish(jnp.dot(s_ref[...], wb, preferred_element_type=F32))


def _matmul_cols(a, small, w, layer, col0, ncols, tm, tn, stacks=None, silu_cols=(), sigmoid_cols=()):
    m, k = a.shape
    ms = small.shape[0]
    c0 = col0 // tn
    grid = (m // tm, ncols // tn)
    assert all(c % tn == 0 for rng in (*silu_cols, *sigmoid_cols) for c in rng)
    tiles = lambda cols: tuple((first // tn, end // tn) for first, end in cols)
    body = functools.partial(_matmul_kernel, silu_tiles=tiles(silu_cols),
                             sigmoid_tiles=tiles(sigmoid_cols))
    small_col = lambda i, j: jnp.where(i == 0, j, grid[1] - 1)
    in_specs = [pl.BlockSpec((tm, k), lambda i, j: (i, 0)),
                pl.BlockSpec((None, k, tn), lambda i, j: (layer, 0, c0 + j)),
                pl.BlockSpec((ms, k), lambda i, j: (0, 0))]
    args = [a, w, small]
    aliases = {}
    if stacks is None:
        out_specs = [pl.BlockSpec((tm, tn), lambda i, j: (i, j)),
                     pl.BlockSpec((ms, tn), lambda i, j: (0, small_col(i, j)))]
        out_shape = [jax.ShapeDtypeStruct((m, ncols), F32), jax.ShapeDtypeStruct((ms, ncols), F32)]
    else:
        out_specs = [pl.BlockSpec((None, tm, tn), lambda i, j: (layer, i, j)),
                     pl.BlockSpec((None, ms, tn), lambda i, j: (layer, 0, small_col(i, j)))]
        out_shape = [jax.ShapeDtypeStruct(buf.shape, F32) for buf in stacks]
        in_specs += [pl.BlockSpec(memory_space=pl.ANY)] * 2
        args += list(stacks)
        aliases = {3: 0, 4: 1}
    return pl.pallas_call(
        body,
        grid=grid,
        in_specs=in_specs,
        out_specs=out_specs,
        out_shape=out_shape,
        input_output_aliases=aliases,
        compiler_params=_params("arbitrary", "arbitrary"),
        name="in_proj",
    )(*args)


def _t5_bias(dist, rb_ref, h):
    max_exact = N_BUCKETS // 2
    d = jnp.maximum(dist, 0)
    ratio = jnp.log(jnp.maximum(d, 1).astype(F32) / max_exact) / math.log(MAX_DISTANCE / max_exact)
    large = max_exact + (ratio * (N_BUCKETS - max_exact)).astype(I32)
    large = jnp.minimum(large, N_BUCKETS - 1)
    bucket = jnp.where(d < max_exact, d, large)
    bias = jnp.zeros(dist.shape, F32)
    for b in range(N_BUCKETS):
        bias = jnp.where(bucket == b, rb_ref[h, b], bias)
    return bias


def _split_bf16(x):
    hi = x.astype(BF16)
    lo = (x - hi.astype(F32)).astype(BF16)
    return hi, lo


_NT = (((1,), (1,)), ((), ()))


def _prompt_attn_kernel(pt_ref, rb_ref, q_ref, k_ref, v_ref, z_ref, *refs, share):
    qb_ref, kb_ref, vt_ref, sel_ref, bias_ref, s_ref, p_ref = refs[-7:]
    o_ref, km_ref = refs[-9:-7]
    h, b = pl.program_id(0), pl.program_id(1)
    share.reduce(h * pl.num_programs(1) + b, refs[:-9], km_ref)

    t = q_ref.shape[0]
    nb = t // MOBA_BLOCK
    blk = MOBA_BLOCK
    key = lax.broadcasted_iota(I32, (blk, blk), 0)
    qry = lax.broadcasted_iota(I32, (blk, blk), 1)

    @pl.when(b == 0)
    def _():
        bias_ref[0] = _t5_bias(qry - key, rb_ref, h)
        bias_ref[1] = _t5_bias(qry - key + blk, rb_ref, h)
    far_bias = rb_ref[h, N_BUCKETS - 1]

    kf = k_ref[...]
    qs = q_ref[...] * (HEAD_DIM ** -0.5)
    qh, ql = _split_bf16(qs)
    qb_ref[...] = qh
    kb_ref[...] = kf.astype(BF16)
    for j in range(nb):
        cols = slice(j * blk, (j + 1) * blk)
        vt_ref[:, cols] = v_ref[cols, :].T.astype(BF16)

    kmean = jnp.mean(kf.reshape(nb, blk, HEAD_DIM), axis=1)
    nbp = sel_ref.shape[0]
    kmean = jnp.concatenate([kmean, jnp.zeros((nbp - nb, HEAD_DIM), F32)], axis=0)
    kh, kl = _split_bf16(kmean)
    score = (lax.dot_general(kh, qh, _NT, preferred_element_type=F32)
             + lax.dot_general(kl, qh, _NT, preferred_element_type=F32)
             + lax.dot_general(kh, ql, _NT, preferred_element_type=F32))
    blk_id = lax.broadcasted_iota(I32, (nbp, t), 0)
    own = lax.broadcasted_iota(I32, (nbp, t), 1) // blk
    past = blk_id < own
    blk_f = blk_id.astype(F32)
    sc = jnp.where(past, score, NEG_INF)
    picked = jnp.zeros((nbp, t), F32)
    for _ in range(min(MOBA_TOPK, nb)):
        best = jnp.max(sc, axis=0, keepdims=True)
        first = jnp.min(jnp.where(sc == best, blk_f, float(nbp)), axis=0, keepdims=True)
        hit = blk_f == first
        picked = jnp.where(hit, 1.0, picked)
        sc = jnp.where(hit, -jnp.inf, sc)
    sel_ref[...] = jnp.where(past, picked, 0.0)

    for i in range(nb):
        rows = slice(i * blk, (i + 1) * blk)
        qi = qb_ref[rows, :]
        m = None
        for j in range(i + 1):
            cols = slice(j * blk, (j + 1) * blk)
            s = lax.dot_general(kb_ref[cols, :], qi, _NT, preferred_element_type=F32)
            if j == i:
                s = jnp.where(key <= qry, s + bias_ref[0], NEG_INF)
            else:
                s = s + (bias_ref[1] if j == i - 1 else far_bias)
                s = jnp.where(sel_ref[j:j + 1, rows] > 0.0, s, NEG_INF)
            s_ref[cols, :] = s
            mj = jnp.max(s, axis=0, keepdims=True)
            m = mj if m is None else jnp.maximum(m, mj)
        l = jnp.zeros((1, blk), F32)
        for j in range(i + 1):
            cols = slice(j * blk, (j + 1) * blk)
            p = jnp.exp(s_ref[cols, :] - m)
            l = l + jnp.sum(p, axis=0, keepdims=True)
            p_ref[cols, :] = p.astype(BF16)
        n = (i + 1) * blk
        acc = jnp.dot(vt_ref[:, 0:n], p_ref[0:n, :], preferred_element_type=F32)
        o_ref[rows, :] = ((acc / l).T * z_ref[rows, :]).astype(o_ref.dtype)


def _prompt_attention(q, k_stack, v_stack, pr, rel_bias, layer, means):
    b, t, d_att = q.shape
    nh = d_att // HEAD_DIM
    nb = t // MOBA_BLOCK
    nbp = -(-nb // BF16_SUBLANE) * BF16_SUBLANE
    cache, page_table, first_page, n_pages = means
    share = _BlockMeanShare(cache, layer, first_page, n_pages, nh * b, lambda hi, bi: hi * b + bi)

    spec = pl.BlockSpec((None, t, HEAD_DIM), lambda hi, bi, pt_ref: (bi, 0, hi))
    kv_spec = pl.BlockSpec((None, None, t, HEAD_DIM), lambda hi, bi, pt_ref: (layer, bi, 0, hi))
    return pl.pallas_call(
        functools.partial(_prompt_attn_kernel, share=share),
        grid_spec=pltpu.PrefetchScalarGridSpec(
            num_scalar_prefetch=1,
            grid=(nh, b),
            in_specs=[pl.BlockSpec(memory_space=pltpu.SMEM), spec, kv_spec, kv_spec, spec]
                     + share.in_specs,
            out_specs=[spec, share.out_spec],
            scratch_shapes=[pltpu.VMEM((t, HEAD_DIM), BF16),
                            pltpu.VMEM((t, HEAD_DIM), BF16),
                            pltpu.VMEM((HEAD_DIM, t), BF16),
                            pltpu.VMEM((nbp, t), F32),
                            pltpu.VMEM((2, MOBA_BLOCK, MOBA_BLOCK), F32),
                            pltpu.VMEM((t, MOBA_BLOCK), F32),
                            pltpu.VMEM((t, MOBA_BLOCK), BF16)],
        ),
        out_shape=[jax.ShapeDtypeStruct((b, t, d_att), BF16), share.out_shape],
        compiler_params=_params("arbitrary", "arbitrary"),
        name="prompt_attn",
    )(page_table.reshape(-1), rel_bias, q, k_stack, v_stack, pr, *share.args)


def _conv_kernel(scw_ref, cfw_ref, cfb_ref, lng_ref, lnb_ref, psc_ref, pcf_ref,
                 sc_in_ref, sc_b_ref, sc_c_ref, z_sc_ref, cf_a_ref, cf_g_ref, z_cf_ref,
                 mix_ref, sc_state_ref, cf_state_ref, u_ext, g_ext, y_scr):
    tt, d_sc = sc_in_ref.shape
    d_cf = cf_a_ref.shape[1]

    @pl.when(pl.program_id(1) == 0)
    def _():
        g_ext[...] = jnp.zeros_like(g_ext)
        u_ext[0:SC_HALO, :] = psc_ref[...]
        g_ext[0:CF_HALO, :] = pcf_ref[...]

    u_ext[SC_HALO:SC_HALO + tt, :] = sc_c_ref[...] * sc_in_ref[...]
    g_ext[CF_HALO:CF_HALO + tt, :] = cf_a_ref[...] * cf_g_ref[...]

    sc_y = jnp.zeros((tt, d_sc), F32)
    for kk in range(SC_WIDTH):
        off = SC_HALO - (SC_WIDTH - 1) + kk
        sc_y = sc_y + scw_ref[kk:kk + 1, :] * u_ext[off:off + tt, :]
    mix_ref[:, 0:d_sc] = (sc_b_ref[...] * sc_y * z_sc_ref[...]).astype(mix_ref.dtype)

    first = CF_HALO - (CF_WIDTH - 1)
    sub = min(tt, CONV_ROWS)
    for c in range(d_cf // LANE):
        lanes = slice(c * LANE, (c + 1) * LANE)
        for r0 in range(0, tt, sub):
            acc = jnp.zeros((sub, LANE), F32)
            for rem in range(SUBLANE):
                part = None
                for base in range(0, CF_HALO + SUBLANE, SUBLANE):
                    kk = base + rem - first
                    if 0 <= kk < CF_WIDTH:
                        term = (cfw_ref[kk:kk + 1, lanes]
                                * g_ext[r0 + base:r0 + base + sub + SUBLANE, lanes])
                        part = term if part is None else part + term
                if part is not None:
                    acc = acc + part[rem:rem + sub, :]
            y_scr[r0:r0 + sub, lanes] = acc + cfb_ref[:, lanes]
    y = y_scr[...]
    mu = jnp.mean(y, axis=-1, keepdims=True)
    var = jnp.mean(jnp.square(y - mu), axis=-1, keepdims=True)
    ln = (y - mu) * lax.rsqrt(var + LN_EPS) * lng_ref[...] + lnb_ref[...]
    mix_ref[:, d_sc:d_sc + d_cf] = (_silu(ln) * z_cf_ref[...]).astype(mix_ref.dtype)

    sc_tail = u_ext[tt:tt + SC_HALO, :]
    cf_tail = g_ext[tt:tt + CF_HALO, :]
    sc_state_ref[...] = sc_tail
    cf_state_ref[...] = cf_tail
    u_ext[0:SC_HALO, :] = sc_tail
    g_ext[0:CF_HALO, :] = cf_tail


def _conv_mixers(pr, prev_sc, prev_cf, sc_w, cf_w, cf_b, ln_g, ln_b, d_att, d_sc, d_cf, tt):
    b, t, _ = pr.shape
    assert d_sc == d_cf and tt % min(tt, CONV_ROWS) == 0
    c0 = d_att // d_sc
    col = lambda n: pl.BlockSpec((None, tt, d_sc), lambda bi, ti, n=n: (bi, ti, c0 + n))
    full = lambda a: pl.BlockSpec(a.shape, lambda bi, ti: (0,) * a.ndim)
    state = lambda rows, d: pl.BlockSpec((None, rows, d), lambda bi, ti: (bi, 0, 0))
    cf_b, ln_g, ln_b = (a.reshape(1, d_cf) for a in (cf_b, ln_g, ln_b))
    ext_rows = lambda halo: -(-(halo + tt) // SUBLANE) * SUBLANE + SUBLANE
    return pl.pallas_call(
        _conv_kernel,
        grid=(b, t // tt),
        in_specs=[full(sc_w), full(cf_w), full(cf_b), full(ln_g), full(ln_b),
                  state(SC_HALO, d_sc), state(CF_HALO, d_cf)] + [col(n) for n in range(7)],
        out_specs=[pl.BlockSpec((None, tt, d_sc + d_cf), lambda bi, ti: (bi, ti, 0)),
                   state(SC_HALO, d_sc), state(CF_HALO, d_cf)],
        out_shape=[jax.ShapeDtypeStruct((b, t, d_sc + d_cf), BF16),
                   jax.ShapeDtypeStruct((b, SC_HALO, d_sc), F32),
                   jax.ShapeDtypeStruct((b, CF_HALO, d_cf), F32)],
        scratch_shapes=[pltpu.VMEM((ext_rows(SC_HALO), d_sc), F32),
                        pltpu.VMEM((ext_rows(CF_HALO), d_cf), F32),
                        pltpu.VMEM((tt, d_cf), F32)],
        compiler_params=_params("parallel", "arbitrary"),
        name="conv_mixers",
    )(sc_w, cf_w, cf_b, ln_g, ln_b, prev_sc, prev_cf, *([pr] * 7))


def _pad_history(state, halo):
    return jnp.pad(state, ((0, 0), (halo - state.shape[1], 0), (0, 0)))


def _sample_topk_kernel(q_ref, km_ref, o_ref):
    nq = q_ref.shape[0]
    nh, nb, _ = km_ref.shape
    blk_f = lax.broadcasted_iota(I32, (nb, 1), 0).astype(F32)
    row = lax.broadcasted_iota(I32, (nq, LANE), 0)
    lane = lax.broadcasted_iota(I32, (nq, LANE), 1)
    for h in range(nh):
        cols = slice(h * HEAD_DIM, (h + 1) * HEAD_DIM)
        km = km_ref[h]
        out = jnp.zeros((nq, LANE), F32)
        for qi in range(nq):
            qv = q_ref[qi:qi + 1, cols] * (HEAD_DIM ** -0.5)
            sc = jnp.sum(km * qv, axis=1, keepdims=True)
            for r in range(MOBA_TOPK):
                best = jnp.max(sc, axis=0, keepdims=True)
                first = jnp.min(jnp.where(sc == best, blk_f, float(nb)), axis=0, keepdims=True)
                out = jnp.where((row == qi) & (lane == r), first, out)
                sc = jnp.where(blk_f == first, -jnp.inf, sc)
        o_ref[h] = out.astype(I32)


def _sample_topk(q, kmean):
    db, t, d_att = q.shape
    _, nh, nb, hd = kmean.shape
    out = pl.pallas_call(
        _sample_topk_kernel,
        grid=(db,),
        in_specs=[pl.BlockSpec((None, t, d_att), lambda bi: (bi, 0, 0)),
                  pl.BlockSpec((None, nh, nb, hd), lambda bi: (bi, 0, 0, 0))],
        out_specs=pl.BlockSpec((None, nh, t, LANE), lambda bi: (bi, 0, 0, 0)),
        out_shape=jax.ShapeDtypeStruct((db, nh, t, LANE), I32),
        compiler_params=_params("parallel"),
        name="sample_topk",
    )(q, kmean)
    return out[..., :MOBA_TOPK]


def _last_page_bias_kernel(rb_ref, o_ref):
    nh, nq, page = o_ref.shape
    qi = lax.broadcasted_iota(I32, (nq, page), 0)
    row = lax.broadcasted_iota(I32, (nq, page), 1)
    for h in range(nh):
        o_ref[h] = _t5_bias(page + qi - row, rb_ref, h)


def _last_page_bias(rel_bias, nq):
    nh = rel_bias.shape[0]
    out = pl.pallas_call(
        _last_page_bias_kernel,
        in_specs=[pl.BlockSpec(memory_space=pltpu.SMEM)],
        out_shape=jax.ShapeDtypeStruct((nh, nq, PAGE_SIZE), F32),
        name="last_page_bias",
    )(rel_bias)
    return out[..., None]


class _SampleGather:
    def __init__(self, topk_ref, pt_ref, ck_hbm, cv_hbm, kbuf, vbuf, sems, *, layer, n_pages, nh, nq):
        self.topk_ref, self.pt_ref = topk_ref, pt_ref
        self.src = (ck_hbm, cv_hbm)
        self.dst = (kbuf, vbuf)
        self.sems = sems
        self.layer, self.n_pages, self.nh, self.nq = layer, n_pages, nh, nq
        self.n_sel = MOBA_TOPK * PAGES_PER_BLOCK
        self.slots = [(qi, s, p) for qi in range(nq) for s in range(MOBA_TOPK)
                      for p in range(PAGES_PER_BLOCK)]

    def chosen_block(self, g, qi, s):
        return self.topk_ref[(g * self.nq + qi) * MOBA_TOPK + s]

    def slot_index(self, qi, s, p):
        return qi * self.n_sel + s * PAGES_PER_BLOCK + p

    def _copies(self, g, half, qi, s, p):
        page = self.pt_ref[(g // self.nh) * self.n_pages
                           + self.chosen_block(g, qi, s) * PAGES_PER_BLOCK + p]
        n = self.slot_index(qi, s, p)
        head = g % self.nh
        return [pltpu.make_async_copy(src.at[self.layer, page, :, head, :], dst.at[half, n],
                                      self.sems.at[which, half, n])
                for which, (src, dst) in enumerate(zip(self.src, self.dst))]

    def start(self, g, half):
        for slot in self.slots:
            for cp in self._copies(g, half, *slot):
                cp.start()

    def wait(self, g, half):
        for slot in self.slots:
            for cp in self._copies(g, half, *slot):
                cp.wait()


def _sample_attend(gather, step, rb_ref, near_ref, q_ref, kn_ref, vn_ref, z_ref, o_ref, *, n_blocks):
    nq = gather.nq
    h = step % gather.nh
    half = step % 2
    kbuf, vbuf = gather.dst
    nrow = lax.broadcasted_iota(I32, (nq, 1), 0)
    far_bias = rb_ref[h, N_BUCKETS - 1]
    for qi in range(nq):
        qv = q_ref[qi:qi + 1, :] * (HEAD_DIM ** -0.5)
        logits = []
        for s in range(MOBA_TOPK):
            blk = gather.chosen_block(step, qi, s)
            for p in range(PAGES_PER_BLOCK):
                kp = kbuf[half, gather.slot_index(qi, s, p)]
                lg = jnp.sum(kp * qv, axis=1, keepdims=True)
                if p == PAGES_PER_BLOCK - 1:
                    bias = jnp.where(blk == n_blocks - 1, near_ref[qi], far_bias)
                else:
                    bias = far_bias
                logits.append((lg + bias, gather.slot_index(qi, s, p)))
        own = jnp.sum(kn_ref[...] * qv, axis=1, keepdims=True) + _t5_bias(qi - nrow, rb_ref, h)
        own = jnp.where(nrow <= qi, own, NEG_INF)
        m = jnp.max(own, axis=0, keepdims=True)
        for lg, _ in logits:
            m = jnp.maximum(m, jnp.max(lg, axis=0, keepdims=True))
        p_own = jnp.exp(own - m)
        l = jnp.sum(p_own, axis=0, keepdims=True)
        acc = jnp.sum(p_own * vn_ref[...], axis=0, keepdims=True)
        for lg, n in logits:
            pn = jnp.exp(lg - m)
            l = l + jnp.sum(pn, axis=0, keepdims=True)
            acc = acc + jnp.sum(pn * vbuf[half, n], axis=0, keepdims=True)
        o_ref[qi:qi + 1, :] = acc / l * z_ref[qi:qi + 1, :]


def _out_proj_step(ma_ref, mc_ref, wa_ref, wc_ref, g_ref, x_hbm, o_ref, ss_ref, x_buf, x_sem):
    i, j = pl.program_id(0), pl.program_id(1)
    tm = o_ref.shape[0]
    tn = wa_ref.shape[1]

    def x_copy():
        return pltpu.make_async_copy(x_hbm.at[pl.ds(pl.multiple_of(i * tm, tm), tm), :], x_buf, x_sem)

    @pl.when(j == 0)
    def _():
        ss_ref[...] = jnp.zeros_like(ss_ref)
        x_copy().start()

    y = (jnp.dot(ma_ref[...].astype(BF16), wa_ref[...], preferred_element_type=F32)
         + jnp.dot(mc_ref[...].astype(BF16), wc_ref[...], preferred_element_type=F32))
    ss_ref[...] += jnp.sum(y * y, axis=-1, keepdims=True)
    o_ref[:, pl.ds(pl.multiple_of(j * tn, tn), tn)] = y

    @pl.when(j == pl.num_programs(1) - 1)
    def _():
        x_copy().wait()
        inv = lax.rsqrt(ss_ref[...] / o_ref.shape[1] + RMS_EPS)
        o_ref[...] = x_buf[...] + o_ref[...] * inv * g_ref[...]


def _out_proj_kernel(*refs):
    _out_proj_step(*refs)


def _out_proj_hosting_kernel(topk_ref, pt_ref, ma_ref, mc_ref, wa_ref, wc_ref, g_ref, x_hbm,
                             rb_ref, near_ref, q_ref, kn_ref, vn_ref, z_ref, ck_hbm, cv_hbm,
                             o_ref, mix_ref, ss_ref, x_buf, x_sem, kbuf, vbuf, sems,
                             *, layer, n_pages, nh):
    step = pl.program_id(0) * pl.num_programs(1) + pl.program_id(1)
    n_steps = pl.num_programs(0) * pl.num_programs(1)
    half = step % 2
    gather = _SampleGather(topk_ref, pt_ref, ck_hbm, cv_hbm, kbuf, vbuf, sems,
                           layer=layer, n_pages=n_pages, nh=nh, nq=q_ref.shape[0])

    @pl.when(step == 0)
    def _():
        gather.start(step, half)

    @pl.when(step + 1 < n_steps)
    def _():
        gather.start(step + 1, 1 - half)

    _out_proj_step(ma_ref, mc_ref, wa_ref, wc_ref, g_ref, x_hbm, o_ref, ss_ref, x_buf, x_sem)

    gather.wait(step, half)
    _sample_attend(gather, step, rb_ref, near_ref, q_ref, kn_ref, vn_ref, z_ref, mix_ref,
                   n_blocks=n_pages // PAGES_PER_BLOCK)


def _out_proj(mix_att, mix_conv, w_out, layer, x, g, tm, tn, sample=None):
    m, d = x.shape
    ka, kc = mix_att.shape[1], mix_conv.shape[1]
    assert ka == kc
    grid = (m // tm, d // tn)
    in_specs = [pl.BlockSpec((tm, ka), lambda i, j, *_: (i, 0)),
                pl.BlockSpec((tm, kc), lambda i, j, *_: (i, 0)),
                pl.BlockSpec((None, ka, tn), lambda i, j, *_: (layer, 0, j)),
                pl.BlockSpec((None, kc, tn), lambda i, j, *_: (layer, 1, j)),
                pl.BlockSpec((1, d), lambda i, j, *_: (0, 0)),
                pl.BlockSpec(memory_space=pl.ANY)]
    out_spec = pl.BlockSpec((tm, d), lambda i, j, *_: (i, 0))
    out_shape = jax.ShapeDtypeStruct((m, d), F32)
    scratch = [pltpu.VMEM((tm, 1), F32), pltpu.VMEM((tm, d), F32), pltpu.SemaphoreType.DMA(())]
    args = (mix_att, mix_conv, w_out, w_out, g.reshape(1, d), x)
    if sample is None:
        return pl.pallas_call(
            _out_proj_kernel, grid=grid, in_specs=in_specs, out_specs=out_spec, out_shape=out_shape,
            scratch_shapes=scratch, compiler_params=_params("arbitrary", "arbitrary"),
            name="out_proj",
        )(*args)

    q, k_stack, v_stack, pr, cache_k, cache_v, page_table, topk, rel_bias, near_bias = sample
    db, t, d_att = q.shape
    nh = d_att // HEAD_DIM
    n_pages = page_table.shape[1]
    assert (n_pages * PAGE_SIZE) % MOBA_BLOCK == 0 and t <= MOBA_BLOCK
    assert grid[0] * grid[1] == db * nh, "one sample (batch, head) per grid step"
    assert PAGE_SIZE + 1 >= MAX_DISTANCE
    n_buf = t * MOBA_TOPK * PAGES_PER_BLOCK

    def head_map(i, j, *_):
        step = i * grid[1] + j
        return (step // nh, 0, step % nh)

    head = pl.BlockSpec((None, t, HEAD_DIM), head_map)
    new_rows = pl.BlockSpec((None, None, t, HEAD_DIM), lambda i, j, *_: (layer,) + head_map(i, j))
    near = pl.BlockSpec((None, t, PAGE_SIZE, 1), lambda i, j, *_: (head_map(i, j)[2], 0, 0, 0))
    hbm = pl.BlockSpec(memory_space=pl.ANY)
    return pl.pallas_call(
        functools.partial(_out_proj_hosting_kernel, layer=layer, n_pages=n_pages, nh=nh),
        grid_spec=pltpu.PrefetchScalarGridSpec(
            num_scalar_prefetch=2,
            grid=grid,
            in_specs=in_specs + [pl.BlockSpec(memory_space=pltpu.SMEM), near, head, new_rows,
                                 new_rows, head, hbm, hbm],
            out_specs=[out_spec, head],
            scratch_shapes=scratch + [pltpu.VMEM((2, n_buf, PAGE_SIZE, HEAD_DIM), F32),
                                      pltpu.VMEM((2, n_buf, PAGE_SIZE, HEAD_DIM), F32),
                                      pltpu.SemaphoreType.DMA((2, 2, n_buf))],
        ),
        out_shape=[out_shape, jax.ShapeDtypeStruct((db, t, d_att), F32)],
        compiler_params=_params("arbitrary", "arbitrary"),
        name="out_proj_sample_attn",
    )(topk.reshape(-1), page_table.reshape(-1), *args, rel_bias, near_bias, q, k_stack, v_stack, pr,
      cache_k, cache_v)


IN_PROJ_TM, IN_PROJ_TN = 2048, 256
OUT_PROJ_TM, OUT_PROJ_TN = 512, 512
NORM_TM = 256


def _project_in(xp, xs, layer, kv_p, kv_s, w_in, g_pre, d_att, d_sc, d_cf):
    d = xp.shape[-1]
    shp, shs = xp.shape[:2], xs.shape[:2]
    mp, ms = shp[0] * shp[1], shs[0] * shs[1]
    hp = _rmsnorm(xp.reshape(mp, d), g_pre, NORM_TM)
    hs = _rmsnorm(xs.reshape(ms, d), g_pre, ms)
    proj = functools.partial(_matmul_cols, hp, hs, w_in, layer, tm=IN_PROJ_TM, tn=IN_PROJ_TN)
    q_p, q_s = proj(col0=0, ncols=d_att)
    kv = [proj(col0=(n + 1) * d_att, ncols=d_att,
               stacks=(kv_p[n].reshape(-1, mp, d_att), kv_s[n].reshape(-1, ms, d_att)))
          for n in range(2)]
    z_sc0, cf_g0, z_cf0 = d_att + 3 * d_sc, d_att + 4 * d_sc + d_cf, d_att + 4 * d_sc + 2 * d_cf
    pr_p, pr_s = proj(col0=3 * d_att, ncols=w_in.shape[2] - 3 * d_att,
                      silu_cols=((0, d_att), (z_sc0, z_sc0 + d_sc), (z_cf0, z_cf0 + d_cf)),
                      sigmoid_cols=((cf_g0, cf_g0 + d_cf),))
    group = lambda sh, q, n, pr: (q.reshape(*sh, d_att),
                                  tuple(kv[i][n].reshape(-1, *sh, d_att) for i in range(2)),
                                  pr.reshape(*sh, -1))
    return group(shp, q_p, 0, pr_p), group(shs, q_s, 1, pr_s)


def kernel(x_prompt, x_sample, cache_k, cache_v, state_sc, state_cf, page_table, w_in, w_out,
           norm_pre, norm_post, sc_w, cf_w, cf_b, cf_ln_g, cf_ln_b, rel_bias):
    depth = w_in.shape[0]
    b, t, d = x_prompt.shape
    db, dt, _ = x_sample.shape
    d_sc, d_cf = sc_w.shape[2], cf_w.shape[2]
    d_att = (w_in.shape[2] - 4 * d_sc - 3 * d_cf) // 4
    nh, hd = cache_k.shape[3], cache_k.shape[4]
    w_out_b = w_out.astype(BF16)
    near_bias = _last_page_bias(rel_bias, dt)
    n_pages = page_table.shape[1]
    sc_hist = lambda st: st[:, SC_HALO - (SC_WIDTH - 1):]
    cf_hist = lambda st: st[:, CF_HALO - (CF_WIDTH - 1):]

    hp, hs = x_prompt, x_sample
    kv_p = (jnp.zeros((depth, b, t, d_att), F32),) * 2
    kv_s = (jnp.zeros((depth, db, dt, d_att), F32),) * 2
    outs = [[] for _ in range(4)]
    for l in range(depth):
        conv_w = (sc_w[l], cf_w[l], cf_b[l], cf_ln_g[l], cf_ln_b[l], d_att, d_sc, d_cf)
        (q, kv_p, pr), (q_s, kv_s, pr_s) = _project_in(hp, hs, l, kv_p, kv_s, w_in, norm_pre[l],
                                                       d_att, d_sc, d_cf)

        mix_att, kmean = _prompt_attention(q, *kv_p, pr, rel_bias, l,
                                           (cache_k, page_table, 0, db * n_pages))
        kmean = kmean.reshape(db, n_pages // PAGES_PER_BLOCK, nh, hd)
        mix_conv, scp, cfp = _conv_mixers(pr, jnp.zeros((b, SC_HALO, d_sc), F32),
                                          jnp.zeros((b, CF_HALO, d_cf), F32), *conv_w, tt=256)

        topk = _sample_topk(q_s, kmean.transpose(0, 2, 1, 3))
        hp, mix_att_s = _out_proj(
            mix_att.reshape(b * t, d_att), mix_conv.reshape(b * t, -1), w_out_b, l,
            hp.reshape(b * t, d), norm_post[l], OUT_PROJ_TM, OUT_PROJ_TN,
            sample=(q_s, *kv_s, pr_s, cache_k, cache_v, page_table, topk, rel_bias, near_bias))
        hp = hp.reshape(b, t, d)

        mix_conv_s, scs, cfs = _conv_mixers(pr_s, _pad_history(state_sc[l], SC_HALO),
                                            _pad_history(state_cf[l], CF_HALO), *conv_w, tt=dt)
        hs = _out_proj(mix_att_s.reshape(db * dt, d_att), mix_conv_s.reshape(db * dt, -1), w_out_b, l,
                       hs.reshape(db * dt, d), norm_post[l], db * dt, OUT_PROJ_TN).reshape(db, dt, d)
        for lst, val in zip(outs, (sc_hist(scp), sc_hist(scs), cf_hist(cfp), cf_hist(cfs))):
            lst.append(val)

    scp, scs, cfp, cfs = (jnp.stack(o) for o in outs)
    return (hp, hs,
            kv_p[0].reshape(depth, b, t, nh, hd), kv_p[1].reshape(depth, b, t, nh, hd),
            kv_s[0].reshape(depth, db, dt, nh, hd), kv_s[1].reshape(depth, db, dt, nh, hd),
            scp, scs, cfp, cfs)
```

```python
import functools
import math

import jax
import jax.numpy as jnp
from jax import lax
from jax.experimental import pallas as pl
from jax.experimental.pallas import tpu as pltpu

F32 = jnp.float32
BF16 = jnp.bfloat16
I32 = jnp.int32

HEAD_DIM = 128
PAGE_SIZE = 128
SC_WIDTH = 3
CF_WIDTH = 31
MOBA_BLOCK = 256
MOBA_TOPK = 3
N_BUCKETS = 32
MAX_DISTANCE = 128
RMS_EPS = 1e-6
LN_EPS = 1e-5
NEG_INF = -1e30

PAGES_PER_BLOCK = MOBA_BLOCK // PAGE_SIZE
SUBLANE = 8
BF16_SUBLANE = 16
LANE = 128
SC_HALO = -(-(SC_WIDTH - 1) // SUBLANE) * SUBLANE
CF_HALO = -(-(CF_WIDTH - 1) // SUBLANE) * SUBLANE
CONV_ROWS = 128

VMEM_LIMIT = 60 * 1024 * 1024


def _params(*sem):
    return pltpu.CompilerParams(dimension_semantics=sem, vmem_limit_bytes=VMEM_LIMIT)


def _silu(x):
    return x * jax.nn.sigmoid(x)


def _rmsnorm_kernel(x_ref, g_ref, o_ref):
    x = x_ref[...]
    ms = jnp.mean(x * x, axis=-1, keepdims=True)
    o_ref[...] = (x * lax.rsqrt(ms + RMS_EPS) * g_ref[...]).astype(o_ref.dtype)


def _rmsnorm(x, g, tm):
    m, d = x.shape
    return pl.pallas_call(
        _rmsnorm_kernel,
        grid=(m // tm,),
        in_specs=[pl.BlockSpec((tm, d), lambda i: (i, 0)),
                  pl.BlockSpec((1, d), lambda i: (0, 0))],
        out_specs=pl.BlockSpec((tm, d), lambda i: (i, 0)),
        out_shape=jax.ShapeDtypeStruct((m, d), BF16),
        compiler_params=_params("parallel"),
        name="rmsnorm_pre",
    )(x, g.reshape(1, d))


def _block_means(page_refs, o_ref):
    for n in range(o_ref.shape[0]):
        total = jnp.zeros(o_ref.shape[1:], F32)
        for p in range(PAGES_PER_BLOCK):
            total = total + jnp.sum(page_refs[n * PAGES_PER_BLOCK + p][...], axis=0)
        o_ref[n] = total * (1.0 / MOBA_BLOCK)


class _BlockMeanShare:
    def __init__(self, cache, layer, first_page, n_pages, n_steps, step_of):
        _, _, page, nh, hd = cache.shape
        per_step = -(-n_pages // n_steps)
        self.pages_per_step = -(-per_step // PAGES_PER_BLOCK) * PAGES_PER_BLOCK
        assert n_pages % self.pages_per_step == 0
        self.n_host_steps = n_pages // self.pages_per_step
        self.all_steps = self.n_host_steps == n_steps
        blocks_per_step = self.pages_per_step // PAGES_PER_BLOCK
        host_step = lambda *idx: jnp.minimum(step_of(*idx), self.n_host_steps - 1)

        def page_spec(n):
            def imap(*idx):
                *grid_idx, pt_ref = idx
                return (layer, pt_ref[first_page + host_step(*grid_idx) * self.pages_per_step + n],
                        0, 0, 0)
            return pl.BlockSpec((None, None, page, nh, hd), imap)

        self.in_specs = [page_spec(n) for n in range(self.pages_per_step)]
        self.args = [cache] * self.pages_per_step
        self.out_spec = pl.BlockSpec((blocks_per_step, nh, hd),
                                     lambda *idx: (host_step(*idx[:-1]), 0, 0))
        self.out_shape = jax.ShapeDtypeStruct((n_pages // PAGES_PER_BLOCK, nh, hd), F32)

    def reduce(self, step, page_refs, o_ref):
        if self.all_steps:
            _block_means(page_refs, o_ref)
        else:
            pl.when(step < self.n_host_steps)(lambda: _block_means(page_refs, o_ref))


def _matmul_kernel(a_ref, w_ref, s_ref, *rest):
    o_ref, os_ref = rest[-2:]
    wb = w_ref[...].astype(BF16)
    o_ref[...] = jnp.dot(a_ref[...], wb, preferred_element_type=F32)

    @pl.when(pl.program_id(0) == 0)
    def _():
        os_ref[...] = jnp.dot(s_ref[...], wb, preferred_element_type=F32)


def _matmul_cols(a, small, w, layer, col0, ncols, tm, tn, stacks=None):
    m, k = a.shape
    ms = small.shape[0]
    c0 = col0 // tn
    grid = (m // tm, ncols // tn)
    small_col = lambda i, j: jnp.where(i == 0, j, grid[1] - 1)
    in_specs = [pl.BlockSpec((tm, k), lambda i, j: (i, 0)),
                pl.BlockSpec((None, k, tn), lambda i, j: (layer, 0, c0 + j)),
                pl.BlockSpec((ms, k), lambda i, j: (0, 0))]
    args = [a, w, small]
    aliases = {}
    if stacks is None:
        out_specs = [pl.BlockSpec((tm, tn), lambda i, j: (i, j)),
                     pl.BlockSpec((ms, tn), lambda i, j: (0, small_col(i, j)))]
        out_shape = [jax.ShapeDtypeStruct((m, ncols), F32), jax.ShapeDtypeStruct((ms, ncols), F32)]
    else:
        out_specs = [pl.BlockSpec((None, tm, tn), lambda i, j: (layer, i, j)),
                     pl.BlockSpec((None, ms, tn), lambda i, j: (layer, 0, small_col(i, j)))]
        out_shape = [jax.ShapeDtypeStruct(buf.shape, F32) for buf in stacks]
        in_specs += [pl.BlockSpec(memory_space=pl.ANY)] * 2
        args += list(stacks)
        aliases = {3: 0, 4: 1}
    return pl.pallas_call(
        _matmul_kernel,
        grid=grid,
        in_specs=in_specs,
        out_specs=out_specs,
        out_shape=out_shape,
        input_output_aliases=aliases,
        compiler_params=_params("arbitrary", "arbitrary"),
        name="in_proj",
    )(*args)


def _t5_bias(dist, rb_ref, h):
    max_exact = N_BUCKETS // 2
    d = jnp.maximum(dist, 0)
    ratio = jnp.log(jnp.maximum(d, 1).astype(F32) / max_exact) / math.log(MAX_DISTANCE / max_exact)
    large = max_exact + (ratio * (N_BUCKETS - max_exact)).astype(I32)
    large = jnp.minimum(large, N_BUCKETS - 1)
    bucket = jnp.where(d < max_exact, d, large)
    bias = jnp.zeros(dist.shape, F32)
    for b in range(N_BUCKETS):
        bias = jnp.where(bucket == b, rb_ref[h, b], bias)
    return bias


def _split_bf16(x):
    hi = x.astype(BF16)
    lo = (x - hi.astype(F32)).astype(BF16)
    return hi, lo


_NT = (((1,), (1,)), ((), ()))


def _prompt_attn_kernel(pt_ref, rb_ref, q_ref, k_ref, v_ref, z_ref, *refs, share):
    qb_ref, kb_ref, vt_ref, sel_ref, bias_ref, s_ref, p_ref = refs[-7:]
    o_ref, km_ref = refs[-9:-7]
    h, b = pl.program_id(0), pl.program_id(1)
    share.reduce(h * pl.num_programs(1) + b, refs[:-9], km_ref)

    t = q_ref.shape[0]
    nb = t // MOBA_BLOCK
    blk = MOBA_BLOCK
    key = lax.broadcasted_iota(I32, (blk, blk), 0)
    qry = lax.broadcasted_iota(I32, (blk, blk), 1)

    @pl.when(b == 0)
    def _():
        bias_ref[0] = _t5_bias(qry - key, rb_ref, h)
        bias_ref[1] = _t5_bias(qry - key + blk, rb_ref, h)
    far_bias = rb_ref[h, N_BUCKETS - 1]

    kf = k_ref[...]
    qs = q_ref[...] * (HEAD_DIM ** -0.5)
    qh, ql = _split_bf16(qs)
    qb_ref[...] = qh
    kb_ref[...] = kf.astype(BF16)
    for j in range(nb):
        cols = slice(j * blk, (j + 1) * blk)
        vt_ref[:, cols] = v_ref[cols, :].T.astype(BF16)

    kmean = jnp.mean(kf.reshape(nb, blk, HEAD_DIM), axis=1)
    nbp = sel_ref.shape[0]
    kmean = jnp.concatenate([kmean, jnp.zeros((nbp - nb, HEAD_DIM), F32)], axis=0)
    kh, kl = _split_bf16(kmean)
    score = (lax.dot_general(kh, qh, _NT, preferred_element_type=F32)
             + lax.dot_general(kl, qh, _NT, preferred_element_type=F32)
             + lax.dot_general(kh, ql, _NT, preferred_element_type=F32))
    blk_id = lax.broadcasted_iota(I32, (nbp, t), 0)
    own = lax.broadcasted_iota(I32, (nbp, t), 1) // blk
    past = blk_id < own
    blk_f = blk_id.astype(F32)
    sc = jnp.where(past, score, NEG_INF)
    picked = jnp.zeros((nbp, t), F32)
    for _ in range(min(MOBA_TOPK, nb)):
        best = jnp.max(sc, axis=0, keepdims=True)
        first = jnp.min(jnp.where(sc == best, blk_f, float(nbp)), axis=0, keepdims=True)
        hit = blk_f == first
        picked = jnp.where(hit, 1.0, picked)
        sc = jnp.where(hit, -jnp.inf, sc)
    sel_ref[...] = jnp.where(past, picked, 0.0)

    for i in range(nb):
        rows = slice(i * blk, (i + 1) * blk)
        qi = qb_ref[rows, :]
        m = None
        for j in range(i + 1):
            cols = slice(j * blk, (j + 1) * blk)
            s = lax.dot_general(kb_ref[cols, :], qi, _NT, preferred_element_type=F32)
            if j == i:
                s = jnp.where(key <= qry, s + bias_ref[0], NEG_INF)
            else:
                s = s + (bias_ref[1] if j == i - 1 else far_bias)
                s = jnp.where(sel_ref[j:j + 1, rows] > 0.0, s, NEG_INF)
            s_ref[cols, :] = s
            mj = jnp.max(s, axis=0, keepdims=True)
            m = mj if m is None else jnp.maximum(m, mj)
        l = jnp.zeros((1, blk), F32)
        for j in range(i + 1):
            cols = slice(j * blk, (j + 1) * blk)
            p = jnp.exp(s_ref[cols, :] - m)
            l = l + jnp.sum(p, axis=0, keepdims=True)
            p_ref[cols, :] = p.astype(BF16)
        n = (i + 1) * blk
        acc = jnp.dot(vt_ref[:, 0:n], p_ref[0:n, :], preferred_element_type=F32)
        o_ref[rows, :] = ((acc / l).T * _silu(z_ref[rows, :])).astype(o_ref.dtype)


def _prompt_attention(q, k_stack, v_stack, pr, rel_bias, layer, means):
    b, t, d_att = q.shape
    nh = d_att // HEAD_DIM
    nb = t // MOBA_BLOCK
    nbp = -(-nb // BF16_SUBLANE) * BF16_SUBLANE
    cache, page_table, first_page, n_pages = means
    share = _BlockMeanShare(cache, layer, first_page, n_pages, nh * b, lambda hi, bi: hi * b + bi)

    spec = pl.BlockSpec((None, t, HEAD_DIM), lambda hi, bi, pt_ref: (bi, 0, hi))
    kv_spec = pl.BlockSpec((None, None, t, HEAD_DIM), lambda hi, bi, pt_ref: (layer, bi, 0, hi))
    return pl.pallas_call(
        functools.partial(_prompt_attn_kernel, share=share),
        grid_spec=pltpu.PrefetchScalarGridSpec(
            num_scalar_prefetch=1,
            grid=(nh, b),
            in_specs=[pl.BlockSpec(memory_space=pltpu.SMEM), spec, kv_spec, kv_spec, spec]
                     + share.in_specs,
            out_specs=[spec, share.out_spec],
            scratch_shapes=[pltpu.VMEM((t, HEAD_DIM), BF16),
                            pltpu.VMEM((t, HEAD_DIM), BF16),
                            pltpu.VMEM((HEAD_DIM, t), BF16),
                            pltpu.VMEM((nbp, t), F32),
                            pltpu.VMEM((2, MOBA_BLOCK, MOBA_BLOCK), F32),
                            pltpu.VMEM((t, MOBA_BLOCK), F32),
                            pltpu.VMEM((t, MOBA_BLOCK), BF16)],
        ),
        out_shape=[jax.ShapeDtypeStruct((b, t, d_att), BF16), share.out_shape],
        compiler_params=_params("arbitrary", "arbitrary"),
        name="prompt_attn",
    )(page_table.reshape(-1), rel_bias, q, k_stack, v_stack, pr, *share.args)


def _conv_kernel(scw_ref, cfw_ref, cfb_ref, lng_ref, lnb_ref, psc_ref, pcf_ref,
                 sc_in_ref, sc_b_ref, sc_c_ref, z_sc_ref, cf_a_ref, cf_g_ref, z_cf_ref,
                 mix_ref, sc_state_ref, cf_state_ref, u_ext, g_ext, y_scr):
    tt, d_sc = sc_in_ref.shape
    d_cf = cf_a_ref.shape[1]

    @pl.when(pl.program_id(1) == 0)
    def _():
        g_ext[...] = jnp.zeros_like(g_ext)
        u_ext[0:SC_HALO, :] = psc_ref[...]
        g_ext[0:CF_HALO, :] = pcf_ref[...]

    u_ext[SC_HALO:SC_HALO + tt, :] = sc_c_ref[...] * sc_in_ref[...]
    g_ext[CF_HALO:CF_HALO + tt, :] = cf_a_ref[...] * jax.nn.sigmoid(cf_g_ref[...])

    sc_y = jnp.zeros((tt, d_sc), F32)
    for kk in range(SC_WIDTH):
        off = SC_HALO - (SC_WIDTH - 1) + kk
        sc_y = sc_y + scw_ref[kk:kk + 1, :] * u_ext[off:off + tt, :]
    mix_ref[:, 0:d_sc] = (sc_b_ref[...] * sc_y * _silu(z_sc_ref[...])).astype(mix_ref.dtype)

    first = CF_HALO - (CF_WIDTH - 1)
    sub = min(tt, CONV_ROWS)
    for c in range(d_cf // LANE):
        lanes = slice(c * LANE, (c + 1) * LANE)
        for r0 in range(0, tt, sub):
            acc = jnp.zeros((sub, LANE), F32)
            for rem in range(SUBLANE):
                part = None
                for base in range(0, CF_HALO + SUBLANE, SUBLANE):
                    kk = base + rem - first
                    if 0 <= kk < CF_WIDTH:
                        term = (cfw_ref[kk:kk + 1, lanes]
                                * g_ext[r0 + base:r0 + base + sub + SUBLANE, lanes])
                        part = term if part is None else part + term
                if part is not None:
                    acc = acc + part[rem:rem + sub, :]
            y_scr[r0:r0 + sub, lanes] = acc + cfb_ref[:, lanes]
    y = y_scr[...]
    mu = jnp.mean(y, axis=-1, keepdims=True)
    var = jnp.mean(jnp.square(y - mu), axis=-1, keepdims=True)
    ln = (y - mu) * lax.rsqrt(var + LN_EPS) * lng_ref[...] + lnb_ref[...]
    mix_ref[:, d_sc:d_sc + d_cf] = (_silu(ln) * _silu(z_cf_ref[...])).astype(mix_ref.dtype)

    sc_tail = u_ext[tt:tt + SC_HALO, :]
    cf_tail = g_ext[tt:tt + CF_HALO, :]
    sc_state_ref[...] = sc_tail
    cf_state_ref[...] = cf_tail
    u_ext[0:SC_HALO, :] = sc_tail
    g_ext[0:CF_HALO, :] = cf_tail


def _conv_mixers(pr, prev_sc, prev_cf, sc_w, cf_w, cf_b, ln_g, ln_b, d_att, d_sc, d_cf, tt):
    b, t, _ = pr.shape
    assert d_sc == d_cf and tt % min(tt, CONV_ROWS) == 0
    c0 = d_att // d_sc
    col = lambda n: pl.BlockSpec((None, tt, d_sc), lambda bi, ti, n=n: (bi, ti, c0 + n))
    full = lambda a: pl.BlockSpec(a.shape, lambda bi, ti: (0,) * a.ndim)
    state = lambda rows, d: pl.BlockSpec((None, rows, d), lambda bi, ti: (bi, 0, 0))
    cf_b, ln_g, ln_b = (a.reshape(1, d_cf) for a in (cf_b, ln_g, ln_b))
    ext_rows = lambda halo: -(-(halo + tt) // SUBLANE) * SUBLANE + SUBLANE
    return pl.pallas_call(
        _conv_kernel,
        grid=(b, t // tt),
        in_specs=[full(sc_w), full(cf_w), full(cf_b), full(ln_g), full(ln_b),
                  state(SC_HALO, d_sc), state(CF_HALO, d_cf)] + [col(n) for n in range(7)],
        out_specs=[pl.BlockSpec((None, tt, d_sc + d_cf), lambda bi, ti: (bi, ti, 0)),
                   state(SC_HALO, d_sc), state(CF_HALO, d_cf)],
        out_shape=[jax.ShapeDtypeStruct((b, t, d_sc + d_cf), BF16),
                   jax.ShapeDtypeStruct((b, SC_HALO, d_sc), F32),
                   jax.ShapeDtypeStruct((b, CF_HALO, d_cf), F32)],
        scratch_shapes=[pltpu.VMEM((ext_rows(SC_HALO), d_sc), F32),
                        pltpu.VMEM((ext_rows(CF_HALO), d_cf), F32),
                        pltpu.VMEM((tt, d_cf), F32)],
        compiler_params=_params("parallel", "arbitrary"),
        name="conv_mixers",
    )(sc_w, cf_w, cf_b, ln_g, ln_b, prev_sc, prev_cf, *([pr] * 7))


def _pad_history(state, halo):
    return jnp.pad(state, ((0, 0), (halo - state.shape[1], 0), (0, 0)))


def _sample_topk_kernel(q_ref, km_ref, o_ref):
    nq = q_ref.shape[0]
    nh, nb, _ = km_ref.shape
    blk_f = lax.broadcasted_iota(I32, (nb, 1), 0).astype(F32)
    row = lax.broadcasted_iota(I32, (nq, LANE), 0)
    lane = lax.broadcasted_iota(I32, (nq, LANE), 1)
    for h in range(nh):
        cols = slice(h * HEAD_DIM, (h + 1) * HEAD_DIM)
        km = km_ref[h]
        out = jnp.zeros((nq, LANE), F32)
        for qi in range(nq):
            qv = q_ref[qi:qi + 1, cols] * (HEAD_DIM ** -0.5)
            sc = jnp.sum(km * qv, axis=1, keepdims=True)
            for r in range(MOBA_TOPK):
                best = jnp.max(sc, axis=0, keepdims=True)
                first = jnp.min(jnp.where(sc == best, blk_f, float(nb)), axis=0, keepdims=True)
                out = jnp.where((row == qi) & (lane == r), first, out)
                sc = jnp.where(blk_f == first, -jnp.inf, sc)
        o_ref[h] = out.astype(I32)


def _sample_topk(q, kmean):
    db, t, d_att = q.shape
    _, nh, nb, hd = kmean.shape
    out = pl.pallas_call(
        _sample_topk_kernel,
        grid=(db,),
        in_specs=[pl.BlockSpec((None, t, d_att), lambda bi: (bi, 0, 0)),
                  pl.BlockSpec((None, nh, nb, hd), lambda bi: (bi, 0, 0, 0))],
        out_specs=pl.BlockSpec((None, nh, t, LANE), lambda bi: (bi, 0, 0, 0)),
        out_shape=jax.ShapeDtypeStruct((db, nh, t, LANE), I32),
        compiler_params=_params("parallel"),
        name="sample_topk",
    )(q, kmean)
    return out[..., :MOBA_TOPK]


def _last_page_bias_kernel(rb_ref, o_ref):
    nh, nq, page = o_ref.shape
    qi = lax.broadcasted_iota(I32, (nq, page), 0)
    row = lax.broadcasted_iota(I32, (nq, page), 1)
    for h in range(nh):
        o_ref[h] = _t5_bias(page + qi - row, rb_ref, h)


def _last_page_bias(rel_bias, nq):
    nh = rel_bias.shape[0]
    out = pl.pallas_call(
        _last_page_bias_kernel,
        in_specs=[pl.BlockSpec(memory_space=pltpu.SMEM)],
        out_shape=jax.ShapeDtypeStruct((nh, nq, PAGE_SIZE), F32),
        name="last_page_bias",
    )(rel_bias)
    return out[..., None]


class _SampleGather:
    def __init__(self, topk_ref, pt_ref, ck_hbm, cv_hbm, kbuf, vbuf, sems, *, layer, n_pages, nh, nq):
        self.topk_ref, self.pt_ref = topk_ref, pt_ref
        self.src = (ck_hbm, cv_hbm)
        self.dst = (kbuf, vbuf)
        self.sems = sems
        self.layer, self.n_pages, self.nh, self.nq = layer, n_pages, nh, nq
        self.n_sel = MOBA_TOPK * PAGES_PER_BLOCK
        self.slots = [(qi, s, p) for qi in range(nq) for s in range(MOBA_TOPK)
                      for p in range(PAGES_PER_BLOCK)]

    def chosen_block(self, g, qi, s):
        return self.topk_ref[(g * self.nq + qi) * MOBA_TOPK + s]

    def slot_index(self, qi, s, p):
        return qi * self.n_sel + s * PAGES_PER_BLOCK + p

    def _copies(self, g, half, qi, s, p):
        page = self.pt_ref[(g // self.nh) * self.n_pages
                           + self.chosen_block(g, qi, s) * PAGES_PER_BLOCK + p]
        n = self.slot_index(qi, s, p)
        head = g % self.nh
        return [pltpu.make_async_copy(src.at[self.layer, page, :, head, :], dst.at[half, n],
                                      self.sems.at[which, half, n])
                for which, (src, dst) in enumerate(zip(self.src, self.dst))]

    def start(self, g, half):
        for slot in self.slots:
            for cp in self._copies(g, half, *slot):
                cp.start(priority=1)

    def wait(self, g, half):
        for slot in self.slots:
            for cp in self._copies(g, half, *slot):
                cp.wait()


def _sample_attend(gather, step, rb_ref, near_ref, q_ref, kn_ref, vn_ref, z_ref, o_ref, *, n_blocks):
    nq = gather.nq
    h = step % gather.nh
    half = step % 2
    kbuf, vbuf = gather.dst
    nrow = lax.broadcasted_iota(I32, (nq, 1), 0)
    far_bias = rb_ref[h, N_BUCKETS - 1]
    for qi in range(nq):
        qv = q_ref[qi:qi + 1, :] * (HEAD_DIM ** -0.5)
        logits = []
        for s in range(MOBA_TOPK):
            blk = gather.chosen_block(step, qi, s)
            for p in range(PAGES_PER_BLOCK):
                kp = kbuf[half, gather.slot_index(qi, s, p)]
                lg = jnp.sum(kp * qv, axis=1, keepdims=True)
                if p == PAGES_PER_BLOCK - 1:
                    bias = jnp.where(blk == n_blocks - 1, near_ref[qi], far_bias)
                else:
                    bias = far_bias
                logits.append((lg + bias, gather.slot_index(qi, s, p)))
        own = jnp.sum(kn_ref[...] * qv, axis=1, keepdims=True) + _t5_bias(qi - nrow, rb_ref, h)
        own = jnp.where(nrow <= qi, own, NEG_INF)
        m = jnp.max(own, axis=0, keepdims=True)
        for lg, _ in logits:
            m = jnp.maximum(m, jnp.max(lg, axis=0, keepdims=True))
        p_own = jnp.exp(own - m)
        l = jnp.sum(p_own, axis=0, keepdims=True)
        acc = jnp.sum(p_own * vn_ref[...], axis=0, keepdims=True)
        for lg, n in logits:
            pn = jnp.exp(lg - m)
            l = l + jnp.sum(pn, axis=0, keepdims=True)
            acc = acc + jnp.sum(pn * vbuf[half, n], axis=0, keepdims=True)
        o_ref[qi:qi + 1, :] = acc / l * _silu(z_ref[qi:qi + 1, :])


def _out_proj_step(ma_ref, mc_ref, wa_ref, wc_ref, g_ref, x_hbm, o_ref, ss_ref, x_buf, x_sem):
    i, j = pl.program_id(0), pl.program_id(1)
    tm = o_ref.shape[0]
    tn = wa_ref.shape[1]

    def x_copy():
        return pltpu.make_async_copy(x_hbm.at[pl.ds(pl.multiple_of(i * tm, tm), tm), :], x_buf, x_sem)

    @pl.when(j == 0)
    def _():
        ss_ref[...] = jnp.zeros_like(ss_ref)
        x_copy().start()

    y = (jnp.dot(ma_ref[...].astype(BF16), wa_ref[...], preferred_element_type=F32)
         + jnp.dot(mc_ref[...].astype(BF16), wc_ref[...], preferred_element_type=F32))
    ss_ref[...] += jnp.sum(y * y, axis=-1, keepdims=True)
    o_ref[:, pl.ds(pl.multiple_of(j * tn, tn), tn)] = y

    @pl.when(j == pl.num_programs(1) - 1)
    def _():
        x_copy().wait()
        inv = lax.rsqrt(ss_ref[...] / o_ref.shape[1] + RMS_EPS)
        o_ref[...] = x_buf[...] + o_ref[...] * inv * g_ref[...]


def _out_proj_kernel(*refs):
    _out_proj_step(*refs)


def _out_proj_hosting_kernel(topk_ref, pt_ref, ma_ref, mc_ref, wa_ref, wc_ref, g_ref, x_hbm,
                             rb_ref, near_ref, q_ref, kn_ref, vn_ref, z_ref, ck_hbm, cv_hbm,
                             o_ref, mix_ref, ss_ref, x_buf, x_sem, kbuf, vbuf, sems,
                             *, layer, n_pages, nh):
    step = pl.program_id(0) * pl.num_programs(1) + pl.program_id(1)
    n_steps = pl.num_programs(0) * pl.num_programs(1)
    half = step % 2
    gather = _SampleGather(topk_ref, pt_ref, ck_hbm, cv_hbm, kbuf, vbuf, sems,
                           layer=layer, n_pages=n_pages, nh=nh, nq=q_ref.shape[0])

    @pl.when(step == 0)
    def _():
        gather.start(step, half)

    @pl.when(step + 1 < n_steps)
    def _():
        gather.start(step + 1, 1 - half)

    _out_proj_step(ma_ref, mc_ref, wa_ref, wc_ref, g_ref, x_hbm, o_ref, ss_ref, x_buf, x_sem)

    gather.wait(step, half)
    _sample_attend(gather, step, rb_ref, near_ref, q_ref, kn_ref, vn_ref, z_ref, mix_ref,
                   n_blocks=n_pages // PAGES_PER_BLOCK)


def _out_proj(mix_att, mix_conv, w_out, layer, x, g, tm, tn, sample=None):
    m, d = x.shape
    ka, kc = mix_att.shape[1], mix_conv.shape[1]
    assert ka == kc
    grid = (m // tm, d // tn)
    in_specs = [pl.BlockSpec((tm, ka), lambda i, j, *_: (i, 0)),
                pl.BlockSpec((tm, kc), lambda i, j, *_: (i, 0)),
                pl.BlockSpec((None, ka, tn), lambda i, j, *_: (layer, 0, j)),
                pl.BlockSpec((None, kc, tn), lambda i, j, *_: (layer, 1, j)),
                pl.BlockSpec((1, d), lambda i, j, *_: (0, 0)),
                pl.BlockSpec(memory_space=pl.ANY)]
    out_spec = pl.BlockSpec((tm, d), lambda i, j, *_: (i, 0))
    out_shape = jax.ShapeDtypeStruct((m, d), F32)
    scratch = [pltpu.VMEM((tm, 1), F32), pltpu.VMEM((tm, d), F32), pltpu.SemaphoreType.DMA(())]
    args = (mix_att, mix_conv, w_out, w_out, g.reshape(1, d), x)
    if sample is None:
        return pl.pallas_call(
            _out_proj_kernel, grid=grid, in_specs=in_specs, out_specs=out_spec, out_shape=out_shape,
            scratch_shapes=scratch, compiler_params=_params("arbitrary", "arbitrary"),
            name="out_proj",
        )(*args)

    q, k_stack, v_stack, pr, cache_k, cache_v, page_table, topk, rel_bias, near_bias = sample
    db, t, d_att = q.shape
    nh = d_att // HEAD_DIM
    n_pages = page_table.shape[1]
    assert (n_pages * PAGE_SIZE) % MOBA_BLOCK == 0 and t <= MOBA_BLOCK
    assert grid[0] * grid[1] == db * nh, "one sample (batch, head) per grid step"
    assert PAGE_SIZE + 1 >= MAX_DISTANCE
    n_buf = t * MOBA_TOPK * PAGES_PER_BLOCK

    def head_map(i, j, *_):
        step = i * grid[1] + j
        return (step // nh, 0, step % nh)

    head = pl.BlockSpec((None, t, HEAD_DIM), head_map)
    new_rows = pl.BlockSpec((None, None, t, HEAD_DIM), lambda i, j, *_: (layer,) + head_map(i, j))
    near = pl.BlockSpec((None, t, PAGE_SIZE, 1), lambda i, j, *_: (head_map(i, j)[2], 0, 0, 0))
    hbm = pl.BlockSpec(memory_space=pl.ANY)
    return pl.pallas_call(
        functools.partial(_out_proj_hosting_kernel, layer=layer, n_pages=n_pages, nh=nh),
        grid_spec=pltpu.PrefetchScalarGridSpec(
            num_scalar_prefetch=2,
            grid=grid,
            in_specs=in_specs + [pl.BlockSpec(memory_space=pltpu.SMEM), near, head, new_rows,
                                 new_rows, head, hbm, hbm],
            out_specs=[out_spec, head],
            scratch_shapes=scratch + [pltpu.VMEM((2, n_buf, PAGE_SIZE, HEAD_DIM), F32),
                                      pltpu.VMEM((2, n_buf, PAGE_SIZE, HEAD_DIM), F32),
                                      pltpu.SemaphoreType.DMA((2, 2, n_buf))],
        ),
        out_shape=[out_shape, jax.ShapeDtypeStruct((db, t, d_att), F32)],
        compiler_params=_params("arbitrary", "arbitrary"),
        name="out_proj_sample_attn",
    )(topk.reshape(-1), page_table.reshape(-1), *args, rel_bias, near_bias, q, k_stack, v_stack, pr,
      cache_k, cache_v)


IN_PROJ_TM, IN_PROJ_TN = 2048, 256
OUT_PROJ_TM, OUT_PROJ_TN = 512, 512
NORM_TM = 256


def _project_in(xp, xs, layer, kv_p, kv_s, w_in, g_pre, d_att):
    d = xp.shape[-1]
    shp, shs = xp.shape[:2], xs.shape[:2]
    mp, ms = shp[0] * shp[1], shs[0] * shs[1]
    hp = _rmsnorm(xp.reshape(mp, d), g_pre, NORM_TM)
    hs = _rmsnorm(xs.reshape(ms, d), g_pre, ms)
    proj = functools.partial(_matmul_cols, hp, hs, w_in, layer, tm=IN_PROJ_TM, tn=IN_PROJ_TN)
    q_p, q_s = proj(col0=0, ncols=d_att)
    kv = [proj(col0=(n + 1) * d_att, ncols=d_att,
               stacks=(kv_p[n].reshape(-1, mp, d_att), kv_s[n].reshape(-1, ms, d_att)))
          for n in range(2)]
    pr_p, pr_s = proj(col0=3 * d_att, ncols=w_in.shape[2] - 3 * d_att)
    group = lambda sh, q, n, pr: (q.reshape(*sh, d_att),
                                  tuple(kv[i][n].reshape(-1, *sh, d_att) for i in range(2)),
                                  pr.reshape(*sh, -1))
    return group(shp, q_p, 0, pr_p), group(shs, q_s, 1, pr_s)


def kernel(x_prompt, x_sample, cache_k, cache_v, state_sc, state_cf, page_table, w_in, w_out,
           norm_pre, norm_post, sc_w, cf_w, cf_b, cf_ln_g, cf_ln_b, rel_bias):
    depth = w_in.shape[0]
    b, t, d = x_prompt.shape
    db, dt, _ = x_sample.shape
    d_sc, d_cf = sc_w.shape[2], cf_w.shape[2]
    d_att = (w_in.shape[2] - 4 * d_sc - 3 * d_cf) // 4
    nh, hd = cache_k.shape[3], cache_k.shape[4]
    w_out_b = w_out.astype(BF16)
    near_bias = _last_page_bias(rel_bias, dt)
    n_pages = page_table.shape[1]
    sc_hist = lambda st: st[:, SC_HALO - (SC_WIDTH - 1):]
    cf_hist = lambda st: st[:, CF_HALO - (CF_WIDTH - 1):]

    hp, hs = x_prompt, x_sample
    kv_p = (jnp.zeros((depth, b, t, d_att), F32),) * 2
    kv_s = (jnp.zeros((depth, db, dt, d_att), F32),) * 2
    outs = [[] for _ in range(4)]
    for l in range(depth):
        conv_w = (sc_w[l], cf_w[l], cf_b[l], cf_ln_g[l], cf_ln_b[l], d_att, d_sc, d_cf)
        (q, kv_p, pr), (q_s, kv_s, pr_s) = _project_in(hp, hs, l, kv_p, kv_s, w_in, norm_pre[l], d_att)

        mix_att, kmean = _prompt_attention(q, *kv_p, pr, rel_bias, l,
                                           (cache_k, page_table, 0, db * n_pages))
        kmean = kmean.reshape(db, n_pages // PAGES_PER_BLOCK, nh, hd)
        mix_conv, scp, cfp = _conv_mixers(pr, jnp.zeros((b, SC_HALO, d_sc), F32),
                                          jnp.zeros((b, CF_HALO, d_cf), F32), *conv_w, tt=256)

        topk = _sample_topk(q_s, kmean.transpose(0, 2, 1, 3))
        hp, mix_att_s = _out_proj(
            mix_att.reshape(b * t, d_att), mix_conv.reshape(b * t, -1), w_out_b, l,
            hp.reshape(b * t, d), norm_post[l], OUT_PROJ_TM, OUT_PROJ_TN,
            sample=(q_s, *kv_s, pr_s, cache_k, cache_v, page_table, topk, rel_bias, near_bias))
        hp = hp.reshape(b, t, d)

        mix_conv_s, scs, cfs = _conv_mixers(pr_s, _pad_history(state_sc[l], SC_HALO),
                                            _pad_history(state_cf[l], CF_HALO), *conv_w, tt=dt)
        hs = _out_proj(mix_att_s.reshape(db * dt, d_att), mix_conv_s.reshape(db * dt, -1), w_out_b, l,
                       hs.reshape(db * dt, d), norm_post[l], db * dt, OUT_PROJ_TN).reshape(db, dt, d)
        for lst, val in zip(outs, (sc_hist(scp), sc_hist(scs), cf_hist(cfp), cf_hist(cfs))):
            lst.append(val)

    scp, scs, cfp, cfs = (jnp.stack(o) for o in outs)
    return (hp, hs,
            kv_p[0].reshape(depth, b, t, nh, hd), kv_p[1].reshape(depth, b, t, nh, hd),
            kv_s[0].reshape(depth, db, dt, nh, hd), kv_s[1].reshape(depth, db, dt, nh, hd),
            scp, scs, cfp, cfs)
```
